```python
import math
import jax, jax.numpy as jnp
from jax import lax
import numpy as np

D_MODEL = 2048
BATCH = 4
SEQ = 2048
DEPTH = 1
DEC_BATCH = 128
DEC_SEQ = 4
PAST_LEN = 16384
PAGE_SIZE = 128

MIX_WIDTH = D_MODEL
C_CONV = MIX_WIDTH // 2
C_POOL = MIX_WIDTH - C_CONV
CONV_HEADS = 8
CONV_HEAD_DIM = C_CONV // CONV_HEADS
CONV_WIDTH = 31
POOL_WINDOWS = (2, 4, 8, 16)
N_POOL_GROUPS = len(POOL_WINDOWS)
POOL_GROUP = C_POOL // N_POOL_GROUPS
MAX_WINDOW = max(POOL_WINDOWS)
D_FF = 4 * D_MODEL
N_META = 16
EPS = 1e-6

kernel_name = "hymba_conformer_conv_multiscale_pool_decoder_step"


def rmsnorm(x, g):
    xf = x.astype(jnp.float32)
    y = xf * lax.rsqrt(jnp.mean(xf * xf, axis=-1, keepdims=True) + EPS)
    return (y * g.astype(jnp.float32)).astype(x.dtype)


def head_layernorm(u, g, b):
    n, t, c = u.shape
    uf = u.astype(jnp.float32).reshape(n, t, CONV_HEADS, CONV_HEAD_DIM)
    mu = jnp.mean(uf, axis=-1, keepdims=True)
    var = jnp.mean(jnp.square(uf - mu), axis=-1, keepdims=True)
    y = ((uf - mu) * lax.rsqrt(var + EPS)).reshape(n, t, c)
    return (y * g.astype(jnp.float32) + b.astype(jnp.float32)).astype(u.dtype)


def depthwise_causal_conv(u_full, w_dw, b_dw):
    c = u_full.shape[-1]
    y = lax.conv_general_dilated(
        u_full, w_dw[:, None, :].astype(u_full.dtype), window_strides=(1,), padding='VALID',
        dimension_numbers=('NWC', 'WIO', 'NWC'), feature_group_count=c)
    return y + b_dw.astype(y.dtype)


def multiscale_pool(hist, xb, pos0, w_pool, pool_scale):
    L = hist.shape[1]
    T = xb.shape[1]
    z = jnp.concatenate([hist, xb], axis=1).astype(jnp.float32)
    csum = jnp.concatenate([jnp.zeros_like(z[:, :1]), jnp.cumsum(z, axis=1)], axis=1)
    pos = jnp.arange(T, dtype=jnp.float32) + pos0
    xf = xb.astype(jnp.float32)
    outs = []
    for gi, w in enumerate(POOL_WINDOWS):
        sl = slice(gi * POOL_GROUP, (gi + 1) * POOL_GROUP)
        s = csum[:, L + 1:L + 1 + T, sl] - csum[:, L + 1 - w:L + 1 - w + T, sl]
        cnt = jnp.minimum(jnp.float32(w), pos + 1.0)
        d = s / cnt[None, :, None] - xf[..., sl]
        outs.append(jnp.einsum('btc,cd->btd', d, w_pool[gi].astype(jnp.float32)))
    y = jnp.concatenate(outs, axis=-1) * pool_scale.astype(jnp.float32)
    return y.astype(xb.dtype)


def mixer(h, conv_hist, pool_hist, pos0, w_in, w_dw, b_dw, ln_g, ln_b, w_pool, pool_scale, w_out):
    proj = jnp.einsum('btd,de->bte', h, w_in)
    a = proj[..., :C_CONV]
    gate = proj[..., C_CONV:2 * C_CONV]
    xb = proj[..., 2 * C_CONV:]
    u = a * jax.nn.sigmoid(gate)
    u_full = jnp.concatenate([conv_hist.astype(u.dtype), u], axis=1)
    c = depthwise_causal_conv(u_full, w_dw, b_dw)
    c = jax.nn.silu(head_layernorm(c, ln_g, ln_b))
    p = multiscale_pool(pool_hist.astype(xb.dtype), xb, pos0, w_pool, pool_scale)
    y = jnp.einsum('bte,ed->btd', jnp.concatenate([c, p], axis=-1), w_out)
    new_conv = u_full[:, -(CONV_WIDTH - 1):]
    new_pool = jnp.concatenate([pool_hist.astype(xb.dtype), xb], axis=1)[:, -(MAX_WINDOW - 1):]
    return y, new_conv, new_pool


def sq_relu_mlp(h, w_up, w_down):
    a = jax.nn.relu(jnp.einsum('btd,df->btf', h, w_up))
    return jnp.einsum('btf,fd->btd', a * a, w_down)


def setup_inputs(seed: int = 0) -> dict:
    key = jax.random.key(seed)
    ks = jax.random.split(key, 20)
    f32 = jnp.float32
    n = lambda k, s, sc: jax.random.normal(k, s, f32) * sc
    return {
        "x_prompt": n(ks[0], (BATCH, SEQ, D_MODEL), 1.0),
        "x_sample": n(ks[1], (DEC_BATCH, DEC_SEQ, D_MODEL), 1.0),
        "state_conv": n(ks[2], (DEPTH, DEC_BATCH, CONV_WIDTH - 1, C_CONV), 0.5),
        "state_pool": n(ks[3], (DEPTH, DEC_BATCH, MAX_WINDOW - 1, C_POOL), 1.0),
        "meta_tokens": n(ks[4], (N_META, D_MODEL), 1.0),
        "norm_mix_g": 1.0 + n(ks[5], (DEPTH, D_MODEL), 0.02),
        "w_in": n(ks[6], (DEPTH, D_MODEL, 2 * C_CONV + C_POOL), D_MODEL ** -0.5),
        "w_dw": n(ks[7], (DEPTH, CONV_WIDTH, C_CONV), CONV_WIDTH ** -0.5),
        "b_dw": n(ks[8], (DEPTH, C_CONV), 0.02),
        "conv_ln_g": 1.0 + n(ks[9], (DEPTH, C_CONV), 0.02),
        "conv_ln_b": n(ks[10], (DEPTH, C_CONV), 0.02),
        "w_pool": n(ks[11], (DEPTH, N_POOL_GROUPS, POOL_GROUP, POOL_GROUP), POOL_GROUP ** -0.5),
        "pool_scale": 1.0 + n(ks[12], (DEPTH, C_POOL), 0.1),
        "w_out": n(ks[13], (DEPTH, C_CONV + C_POOL, D_MODEL), (C_CONV + C_POOL) ** -0.5),
        "norm_ffn_g": 1.0 + n(ks[14], (DEPTH, D_MODEL), 0.02),
        "w_up": n(ks[15], (DEPTH, D_MODEL, D_FF), D_MODEL ** -0.5),
        "w_down": n(ks[16], (DEPTH, D_FF, D_MODEL), D_FF ** -0.5),
        "final_norm_g": 1.0 + n(ks[17], (D_MODEL,), 0.02),
    }


def reference(x_prompt, x_sample, state_conv, state_pool, meta_tokens, norm_mix_g, w_in, w_dw, b_dw,
              conv_ln_g, conv_ln_b, w_pool, pool_scale, w_out, norm_ffn_g, w_up, w_down, final_norm_g):
    meta = jnp.broadcast_to(meta_tokens.astype(x_prompt.dtype)[None], (x_prompt.shape[0], N_META, D_MODEL))
    xp = jnp.concatenate([meta, x_prompt], axis=1)
    xs = x_sample
    nb = xp.shape[0]
    zero_conv = jnp.zeros((nb, CONV_WIDTH - 1, C_CONV), xp.dtype)
    zero_pool = jnp.zeros((nb, MAX_WINDOW - 1, C_POOL), xp.dtype)
    conv_p, pool_p, conv_s, pool_s = [], [], [], []
    for l in range(DEPTH):
        wl = (w_in[l], w_dw[l], b_dw[l], conv_ln_g[l], conv_ln_b[l], w_pool[l], pool_scale[l], w_out[l])
        yp, cp, pp = mixer(rmsnorm(xp, norm_mix_g[l]), zero_conv, zero_pool, 0, *wl)
        xp = xp + yp
        xp = xp + sq_relu_mlp(rmsnorm(xp, norm_ffn_g[l]), w_up[l], w_down[l])
        ys, cs, ps = mixer(rmsnorm(xs, norm_mix_g[l]), state_conv[l], state_pool[l], PAST_LEN, *wl)
        xs = xs + ys
        xs = xs + sq_relu_mlp(rmsnorm(xs, norm_ffn_g[l]), w_up[l], w_down[l])
        conv_p.append(cp); pool_p.append(pp); conv_s.append(cs); pool_s.append(ps)
    y_prompt = rmsnorm(xp, final_norm_g)[:, N_META:]
    y_sample = rmsnorm(xs, final_norm_g)
    new_conv_prompt = jnp.stack(conv_p, axis=0)
    new_pool_prompt = jnp.stack(pool_p, axis=0)
    new_conv_sample = jnp.stack(conv_s, axis=0)
    new_pool_sample = jnp.stack(pool_s, axis=0)
    return (y_prompt, y_sample, new_conv_prompt, new_pool_prompt, new_conv_sample, new_pool_sample)
```

```python
import functools

import jax
import jax.numpy as jnp
from jax import lax
from jax.experimental import pallas as pl
from jax.experimental.pallas import tpu as pltpu

D_MODEL = 2048
C_CONV = 1024
C_POOL = 1024
CONV_WIDTH = 31
CONV_HIST = CONV_WIDTH - 1
POOL_WINDOWS = (2, 4, 8, 16)
POOL_GROUP = C_POOL // len(POOL_WINDOWS)
POOL_HIST = max(POOL_WINDOWS) - 1
D_FF = 4 * D_MODEL
N_META = 16
EPS = 1e-6

LANES = 128
SUBLANES = 8
N_CTILE = C_CONV // LANES
CONV_HALO = 32
POOL_HALO = 16

T_MIX = 256
R_CONV = 128
S_SMP = 32
TM_MLP = 512
TF_MLP = 512
VMEM_LIMIT = 56 * 1024 * 1024

F32 = jnp.float32
BF16 = jnp.bfloat16


def _rmsnorm(x, g):
    return x * lax.rsqrt(jnp.mean(x * x, axis=-1, keepdims=True) + EPS) * g


def _glu_proj(x, g, win):
    h = _rmsnorm(x, g).astype(BF16)
    proj = jnp.dot(h, win, preferred_element_type=F32)
    u = proj[:, :C_CONV] * jax.nn.sigmoid(proj[:, C_CONV:2 * C_CONV])
    return u, proj[:, 2 * C_CONV:]


def _head_ln_silu(c, g, b):
    mu = jnp.mean(c, axis=-1, keepdims=True)
    d = c - mu
    var = jnp.mean(d * d, axis=-1, keepdims=True)
    y = d * lax.rsqrt(var + EPS) * g + b
    return y * jax.nn.sigmoid(y)


def _mixer_prompt_kernel(x_ref, meta_ref, gmix_ref, win_ref, wdw_ref, bdw_ref, lng_ref, lnb_ref,
                         wpool_ref, pscale_ref, wout_ref,
                         x1_ref, utail_ref, xbtail_ref,
                         ubuf, xbuf, umeta, xbmeta, wb, cbuf, dbuf, cp):
    b = pl.program_id(0)
    t = pl.program_id(1)
    T = T_MIX

    @pl.when((b == 0) & (t == 0))
    def _():
        um, xm = _glu_proj(meta_ref[...], gmix_ref[...], win_ref[...])
        for j in range(N_CTILE):
            umeta[j] = um[:, j * LANES:(j + 1) * LANES]
            xbmeta[j] = xm[:, j * LANES:(j + 1) * LANES]
        for k in range(CONV_WIDTH):
            wb[k] = jnp.broadcast_to(wdw_ref[k:k + 1, :], (SUBLANES, C_CONV))

    @pl.when(t == 0)
    def _():
        for j in range(N_CTILE):
            ubuf[j, 0:CONV_HALO - N_META, :] = jnp.zeros((CONV_HALO - N_META, LANES), F32)
            ubuf[j, CONV_HALO - N_META:CONV_HALO, :] = umeta[j]
            xbuf[j, 0:POOL_HALO, :] = xbmeta[j]

    x = x_ref[0]
    u, xb = _glu_proj(x, gmix_ref[...], win_ref[...])
    for j in range(N_CTILE):
        ubuf[j, CONV_HALO:CONV_HALO + T, :] = u[:, j * LANES:(j + 1) * LANES]
        xbuf[j, POOL_HALO:POOL_HALO + T, :] = xb[:, j * LANES:(j + 1) * LANES]

    n_chunk = T // R_CONV

    def conv_chunk(i, carry):
        j = i // n_chunk
        r0 = pl.multiple_of((i % n_chunk) * R_CONV, R_CONV)
        lane0 = pl.multiple_of(j * LANES, LANES)
        acc = jnp.broadcast_to(bdw_ref[:, pl.ds(lane0, LANES)], (R_CONV, LANES))
        for k in range(CONV_WIDTH):
            wk = wb[k, :, pl.ds(lane0, LANES)]
            uk = ubuf[j, pl.ds(r0 + (CONV_HALO - CONV_HIST) + k, R_CONV), :]
            acc = acc + uk * jnp.tile(wk, (R_CONV // SUBLANES, 1))
        cbuf[j, pl.ds(r0, R_CONV), :] = acc
        return carry

    lax.fori_loop(0, N_CTILE * n_chunk, conv_chunk, 0)

    for j in range(N_CTILE):
        sl = slice(j * LANES, (j + 1) * LANES)
        cp[:, sl] = _head_ln_silu(cbuf[j], lng_ref[:, sl], lnb_ref[:, sl]).astype(BF16)

    for gi, w in enumerate(POOL_WINDOWS):
        def pool_chunk(i, carry, gi=gi, w=w):
            j = gi * (POOL_GROUP // LANES) + i // n_chunk
            r0 = pl.multiple_of((i % n_chunk) * R_CONV, R_CONV)
            cur = xbuf[j, pl.ds(r0 + POOL_HALO, R_CONV), :]
            s = cur
            for d in range(1, w):
                s = s + xbuf[j, pl.ds(r0 + POOL_HALO - d, R_CONV), :]
            dbuf[pl.ds(r0, R_CONV), pl.ds(pl.multiple_of(j * LANES, LANES), LANES)] = (s * (1.0 / w) - cur).astype(BF16)
            return carry
        lax.fori_loop(0, (POOL_GROUP // LANES) * n_chunk, pool_chunk, 0)

    for gi in range(len(POOL_WINDOWS)):
        sl = slice(gi * POOL_GROUP, (gi + 1) * POOL_GROUP)
        p = jnp.dot(dbuf[:, sl], wpool_ref[gi], preferred_element_type=F32) * pscale_ref[:, sl]
        cp[:, C_CONV + gi * POOL_GROUP:C_CONV + (gi + 1) * POOL_GROUP] = p.astype(BF16)

    y = jnp.dot(cp[...], wout_ref[...], preferred_element_type=F32)
    x1_ref[0] = x + y

    @pl.when(t == pl.num_programs(1) - 1)
    def _():
        for j in range(N_CTILE):
            utail_ref[0, :, j * LANES:(j + 1) * LANES] = ubuf[j, T:T + CONV_HALO, :]
            xbtail_ref[0, :, j * LANES:(j + 1) * LANES] = xbuf[j, T:T + POOL_HALO, :]

    for j in range(N_CTILE):
        ubuf[j, 0:CONV_HALO, :] = ubuf[j, T:T + CONV_HALO, :]
        xbuf[j, 0:POOL_HALO, :] = xbuf[j, T:T + POOL_HALO, :]


def _const_spec(shape):
    return pl.BlockSpec(shape, lambda *_: (0,) * len(shape), pipeline_mode=pl.Buffered(1))


def _mixer_prompt(x_prompt, meta, gmix, win, wdw, bdw, lng, lnb, wpool, pscale, wout):
    nb, seq, _ = x_prompt.shape
    T = T_MIX
    return pl.pallas_call(
        _mixer_prompt_kernel,
        grid=(nb, seq // T),
        in_specs=[
            pl.BlockSpec((1, T, D_MODEL), lambda b, t: (b, t, 0)),
            _const_spec((N_META, D_MODEL)),
            _const_spec((1, D_MODEL)),
            _const_spec((D_MODEL, 2 * C_CONV + C_POOL)),
            _const_spec((CONV_WIDTH, C_CONV)),
            _const_spec((1, C_CONV)),
            _const_spec((1, C_CONV)),
            _const_spec((1, C_CONV)),
            _const_spec((len(POOL_WINDOWS), POOL_GROUP, POOL_GROUP)),
            _const_spec((1, C_POOL)),
            _const_spec((C_CONV + C_POOL, D_MODEL)),
        ],
        out_specs=[
            pl.BlockSpec((1, T, D_MODEL), lambda b, t: (b, t, 0)),
            pl.BlockSpec((1, CONV_HALO, C_CONV), lambda b, t: (b, 0, 0)),
            pl.BlockSpec((1, POOL_HALO, C_POOL), lambda b, t: (b, 0, 0)),
        ],
        out_shape=[
            jax.ShapeDtypeStruct((nb, seq, D_MODEL), F32),
            jax.ShapeDtypeStruct((nb, CONV_HALO, C_CONV), F32),
            jax.ShapeDtypeStruct((nb, POOL_HALO, C_POOL), F32),
        ],
        scratch_shapes=[
            pltpu.VMEM((N_CTILE, CONV_HALO + T, LANES), F32),
            pltpu.VMEM((N_CTILE, POOL_HALO + T, LANES), F32),
            pltpu.VMEM((N_CTILE, N_META, LANES), F32),
            pltpu.VMEM((N_CTILE, N_META, LANES), F32),
            pltpu.VMEM((CONV_WIDTH, SUBLANES, C_CONV), F32),
            pltpu.VMEM((N_CTILE, T, LANES), F32),
            pltpu.VMEM((T, C_POOL), BF16),
            pltpu.VMEM((T, C_CONV + C_POOL), BF16),
        ],
        compiler_params=pltpu.CompilerParams(
            dimension_semantics=("arbitrary", "arbitrary"), vmem_limit_bytes=VMEM_LIMIT),
        name="mixer_prompt",
    )(x_prompt, meta, gmix, win, wdw, bdw, lng, lnb, wpool, pscale, wout)


def _mixer_sample_kernel(xs_ref, hist_ref, phist_ref, gmix_ref, win_ref, wdw_ref, bdw_ref, lng_ref, lnb_ref,
                         wpool_ref, pscale_ref, wout_ref,
                         x1_ref, u_ref, xb_ref, cp):
    S = S_SMP
    nj = xs_ref.shape[0]
    x = jnp.concatenate([xs_ref[j] for j in range(nj)], axis=0)
    u, xb = _glu_proj(x, gmix_ref[...], win_ref[...])
    for j in range(nj):
        u_ref[j] = u[j * S:(j + 1) * S]
        xb_ref[j] = xb[j * S:(j + 1) * S]

    def conv_tile(lt, carry):
        lane0 = pl.multiple_of(lt * LANES, LANES)
        lanes = pl.ds(lane0, LANES)
        bias = jnp.broadcast_to(bdw_ref[:, lanes], (S, LANES))
        acc = [bias for _ in range(nj)]
        for kk in range(CONV_HIST + nj):
            row = hist_ref[kk, :, lanes] if kk < CONV_HIST else u_ref[kk - CONV_HIST, :, lanes]
            for j in range(nj):
                k = kk - j
                if 0 <= k < CONV_WIDTH:
                    acc[j] = acc[j] + row * wdw_ref[k:k + 1, lanes]
        for j in range(nj):
            c = _head_ln_silu(acc[j], lng_ref[:, lanes], lnb_ref[:, lanes])
            cp[pl.ds(j * S, S), lanes] = c.astype(BF16)
        return carry

    lax.fori_loop(0, N_CTILE, conv_tile, 0)

    for gi, w in enumerate(POOL_WINDOWS):
        sl = slice(gi * POOL_GROUP, (gi + 1) * POOL_GROUP)
        rows = [phist_ref[i, :, sl] for i in range(POOL_HIST - w + 1, POOL_HIST)] + [xb_ref[j, :, sl] for j in range(nj)]
        ds = []
        for j in range(nj):
            win_rows = rows[j:j + w]
            s = win_rows[0]
            for r in win_rows[1:]:
                s = s + r
            ds.append(s * (1.0 / w) - win_rows[-1])
        d = jnp.concatenate(ds, axis=0).astype(BF16)
        p = jnp.dot(d, wpool_ref[gi], preferred_element_type=F32) * pscale_ref[:, sl]
        cp[:, C_CONV + gi * POOL_GROUP:C_CONV + (gi + 1) * POOL_GROUP] = p.astype(BF16)

    y = jnp.dot(cp[...], wout_ref[...], preferred_element_type=F32)
    x1 = x + y
    for j in range(nj):
        x1_ref[j] = x1[j * S:(j + 1) * S]


def _mixer_sample(xs_t, hist_t, phist_t, gmix, win, wdw, bdw, lng, lnb, wpool, pscale, wout):
    nj, ns, _ = xs_t.shape
    S = S_SMP
    return pl.pallas_call(
        _mixer_sample_kernel,
        grid=(ns // S,),
        in_specs=[
            pl.BlockSpec((nj, S, D_MODEL), lambda s: (0, s, 0)),
            pl.BlockSpec((CONV_HIST, S, C_CONV), lambda s: (0, s, 0)),
            pl.BlockSpec((POOL_HIST, S, C_POOL), lambda s: (0, s, 0)),
            _const_spec((1, D_MODEL)),
            _const_spec((D_MODEL, 2 * C_CONV + C_POOL)),
            _const_spec((CONV_WIDTH, C_CONV)),
            _const_spec((1, C_CONV)),
            _const_spec((1, C_CONV)),
            _const_spec((1, C_CONV)),
            _const_spec((len(POOL_WINDOWS), POOL_GROUP, POOL_GROUP)),
            _const_spec((1, C_POOL)),
            _const_spec((C_CONV + C_POOL, D_MODEL)),
        ],
        out_specs=[
            pl.BlockSpec((nj, S, D_MODEL), lambda s: (0, s, 0)),
            pl.BlockSpec((nj, S, C_CONV), lambda s: (0, s, 0)),
            pl.BlockSpec((nj, S, C_POOL), lambda s: (0, s, 0)),
        ],
        out_shape=[
            jax.ShapeDtypeStruct((nj, ns, D_MODEL), F32),
            jax.ShapeDtypeStruct((nj, ns, C_CONV), F32),
            jax.ShapeDtypeStruct((nj, ns, C_POOL), F32),
        ],
        scratch_shapes=[pltpu.VMEM((nj * S, C_CONV + C_POOL), BF16)],
        compiler_params=pltpu.CompilerParams(
            dimension_semantics=("arbitrary",), vmem_limit_bytes=VMEM_LIMIT),
        name="mixer_sample",
    )(xs_t, hist_t, phist_t, gmix, win, wdw, bdw, lng, lnb, wpool, pscale, wout)


def _mlp_kernel(x_ref, gffn_ref, wup_ref, wdown_ref, gfin_ref, o_ref, h_ref):
    j = pl.program_id(1)

    @pl.when(j == 0)
    def _():
        x = x_ref[...]
        h_ref[...] = _rmsnorm(x, gffn_ref[...]).astype(BF16)
        o_ref[...] = x

    a = jnp.maximum(jnp.dot(h_ref[...], wup_ref[...], preferred_element_type=F32), 0.0)
    o_ref[...] += jnp.dot((a * a).astype(BF16), wdown_ref[...], preferred_element_type=F32)

    @pl.when(j == pl.num_programs(1) - 1)
    def _():
        o_ref[...] = _rmsnorm(o_ref[...], gfin_ref[...])


def _mlp(x1, gffn, wup, wdown, gfin):
    n, _ = x1.shape
    tm = min(TM_MLP, n)
    return pl.pallas_call(
        _mlp_kernel,
        grid=(n // tm, D_FF // TF_MLP),
        in_specs=[
            pl.BlockSpec((tm, D_MODEL), lambda i, j: (i, 0)),
            _const_spec((1, D_MODEL)),
            pl.BlockSpec((D_MODEL, TF_MLP), lambda i, j: (0, j)),
            pl.BlockSpec((TF_MLP, D_MODEL), lambda i, j: (j, 0)),
            _const_spec((1, D_MODEL)),
        ],
        out_specs=pl.BlockSpec((tm, D_MODEL), lambda i, j: (i, 0)),
        out_shape=jax.ShapeDtypeStruct((n, D_MODEL), F32),
        scratch_shapes=[pltpu.VMEM((tm, D_MODEL), BF16)],
        compiler_params=pltpu.CompilerParams(
            dimension_semantics=("arbitrary", "arbitrary"), vmem_limit_bytes=VMEM_LIMIT),
        name="mlp",
    )(x1, gffn, wup, wdown, gfin)


def kernel(x_prompt, x_sample, state_conv, state_pool, meta_tokens, norm_mix_g, w_in, w_dw, b_dw, conv_ln_g, conv_ln_b, w_pool, pool_scale, w_out, norm_ffn_g, w_up, w_down, final_norm_g):
    assert norm_mix_g.shape[0] == 1, "single-layer step"
    nb, seq, _ = x_prompt.shape
    ns, nj, _ = x_sample.shape

    gmix = norm_mix_g[0][None]
    gffn = norm_ffn_g[0][None]
    gfin = final_norm_g[None]
    win = w_in[0].astype(BF16)
    wout = w_out[0].astype(BF16)
    wpool = w_pool[0].astype(BF16)
    wup = w_up[0].astype(BF16)
    wdown = w_down[0].astype(BF16)
    mix_w = (gmix, win, w_dw[0], b_dw[0][None], conv_ln_g[0][None], conv_ln_b[0][None], wpool, pool_scale[0][None], wout)

    x1p, utail, xbtail = _mixer_prompt(x_prompt, meta_tokens, *mix_w)
    y_prompt = _mlp(x1p.reshape(nb * seq, D_MODEL), gffn, wup, wdown, gfin).reshape(nb, seq, D_MODEL)
    new_conv_prompt = utail[:, CONV_HALO - CONV_HIST:][None]
    new_pool_prompt = xbtail[:, POOL_HALO - POOL_HIST:][None]

    xs_t = jnp.transpose(x_sample, (1, 0, 2))
    hist_t = jnp.transpose(state_conv[0], (1, 0, 2))
    phist_t = jnp.transpose(state_pool[0], (1, 0, 2))
    x1s_t, u_t, xb_t = _mixer_sample(xs_t, hist_t, phist_t, *mix_w)
    ys_t = _mlp(x1s_t.reshape(nj * ns, D_MODEL), gffn, wup, wdown, gfin).reshape(nj, ns, D_MODEL)
    y_sample = jnp.transpose(ys_t, (1, 0, 2))
    new_conv_sample = jnp.concatenate([state_conv[0][:, nj:], jnp.transpose(u_t, (1, 0, 2))], axis=1)[None]
    new_pool_sample = jnp.concatenate([state_pool[0][:, nj:], jnp.transpose(xb_t, (1, 0, 2))], axis=1)[None]
    return (y_prompt, y_sample, new_conv_prompt, new_pool_prompt, new_conv_sample, new_pool_sample)
```

```python
import jax
import jax.numpy as jnp
from jax import lax
from jax.experimental import pallas as pl
from jax.experimental.pallas import tpu as pltpu

D_MODEL = 2048
C_CONV = 1024
C_POOL = 1024
CONV_WIDTH = 31
CONV_HIST = CONV_WIDTH - 1
POOL_WINDOWS = (2, 4, 8, 16)
N_GRP = len(POOL_WINDOWS)
GRP = C_POOL // N_GRP
POOL_HIST = max(POOL_WINDOWS) - 1
D_FF = 4 * D_MODEL
N_META = 16
EPS = 1e-6

LANES = 128
SUBLANES = 8
N_CTILE = C_CONV // LANES
TILES_PER_GRP = GRP // LANES
CONV_HALO = 32
POOL_HALO = 16

T_MIX = 256
R_CONV = 64
KC_MIX = 256
S_SMP = 32
TM_MLP = 1024
TF_MLP = 512
VMEM_LIMIT = 56 * 1024 * 1024

F32 = jnp.float32
BF16 = jnp.bfloat16


def _rmsnorm(x, g):
    return x * lax.rsqrt(jnp.mean(x * x, axis=-1, keepdims=True) + EPS) * g


def _glu(h, wag_q):
    proj = jnp.dot(h, wag_q, preferred_element_type=F32)
    return proj[:, :GRP] * jax.nn.sigmoid(proj[:, GRP:])


def _head_ln_silu(c, g, b):
    mu = jnp.mean(c, axis=-1, keepdims=True)
    d = c - mu
    var = jnp.mean(d * d, axis=-1, keepdims=True)
    y = d * lax.rsqrt(var + EPS) * g + b
    return y * jax.nn.sigmoid(y)


def _lane_tile(j):
    return slice(j * LANES, (j + 1) * LANES)


def _mixer_prompt_kernel(x_ref, meta_ref, gmix_ref, wag_ref, wxb_ref, wdw_ref, bdw_ref, lng_ref, lnb_ref,
                         wpool_ref, pscale_ref, wout_ref,
                         x1_ref, utail_ref, xbtail_ref,
                         hbuf, umeta, xbmeta, wb, *grp_scratch):
    b = pl.program_id(0)
    t = pl.program_id(1)
    T = T_MIX
    ubuf, xbuf, dbuf, cbuf, pbuf = (grp_scratch[i * N_GRP:(i + 1) * N_GRP] for i in range(5))

    def put(dst, v, row0):
        for jj in range(TILES_PER_GRP):
            dst[jj, row0:row0 + v.shape[0], :] = v[:, _lane_tile(jj)]

    @pl.when((b == 0) & (t == 0))
    def _():
        hm = _rmsnorm(meta_ref[...], gmix_ref[...]).astype(BF16)
        xm = jnp.dot(hm, wxb_ref[...], preferred_element_type=F32)
        for q in range(N_GRP):
            put(umeta.at[q], _glu(hm, wag_ref[q]), 0)
            put(xbmeta.at[q], xm[:, q * GRP:(q + 1) * GRP], 0)
        for k in range(CONV_WIDTH):
            wb[k] = jnp.broadcast_to(wdw_ref[k:k + 1, :], (SUBLANES, C_CONV))

    @pl.when(t == 0)
    def _():
        for q in range(N_GRP):
            ubuf[q][:, 0:CONV_HALO - N_META, :] = jnp.zeros((TILES_PER_GRP, CONV_HALO - N_META, LANES), F32)
            ubuf[q][:, CONV_HALO - N_META:CONV_HALO, :] = umeta[q]
            xbuf[q][:, 0:POOL_HALO, :] = xbmeta[q]

    def proj_xb():
        xb = jnp.dot(hbuf[...], wxb_ref[...], preferred_element_type=F32)
        for q in range(N_GRP):
            put(xbuf[q], xb[:, q * GRP:(q + 1) * GRP], POOL_HALO)

    def proj_u(q):
        acc = None
        for k0 in range(0, D_MODEL, KC_MIX):
            d = jnp.dot(hbuf[:, k0:k0 + KC_MIX], wag_ref[q, k0:k0 + KC_MIX, :], preferred_element_type=F32)
            acc = d if acc is None else acc + d
            yield
        put(ubuf[q], acc[:, :GRP] * jax.nn.sigmoid(acc[:, GRP:]), CONV_HALO)
        yield

    def conv(q):
        for jj in range(TILES_PER_GRP):
            lanes = _lane_tile(q * TILES_PER_GRP + jj)
            for r0 in range(0, T, R_CONV):
                acc = jnp.broadcast_to(bdw_ref[:, lanes], (R_CONV, LANES))
                for k in range(CONV_WIDTH):
                    row = r0 + (CONV_HALO - CONV_HIST) + k
                    acc = acc + ubuf[q][jj, row:row + R_CONV, :] * jnp.tile(wb[k, :, lanes], (R_CONV // SUBLANES, 1))
                c = _head_ln_silu(acc, lng_ref[:, lanes], lnb_ref[:, lanes])
                cbuf[q][r0:r0 + R_CONV, _lane_tile(jj)] = c.astype(BF16)
                yield

    def pool(q):
        w = POOL_WINDOWS[q]
        for jj in range(TILES_PER_GRP):
            for r0 in range(0, T, R_CONV):
                cur = xbuf[q][jj, POOL_HALO + r0:POOL_HALO + r0 + R_CONV, :]
                s = cur
                for d in range(1, w):
                    s = s + xbuf[q][jj, POOL_HALO + r0 - d:POOL_HALO + r0 - d + R_CONV, :]
                dbuf[q][r0:r0 + R_CONV, _lane_tile(jj)] = (s * (1.0 / w) - cur).astype(BF16)
            yield

    def pool_proj(q):
        p = jnp.dot(dbuf[q][...], wpool_ref[q], preferred_element_type=F32) * pscale_ref[:, q * GRP:(q + 1) * GRP]
        pbuf[q][...] = p.astype(BF16)
        yield

    def wout(bufs, k0):
        acc = x1_ref[0]
        for i, buf in enumerate(bufs):
            acc = acc + jnp.dot(buf[...], wout_ref[k0 + i * GRP:k0 + (i + 1) * GRP, :], preferred_element_type=F32)
            yield
        x1_ref[0] = acc
        yield

    def seq(*gens):
        for g in gens:
            yield from g

    def interleave(a, b):
        live = [a, b]
        while live:
            for g in list(live):
                if next(g, StopIteration) is StopIteration:
                    live.remove(g)

    x = x_ref[0]
    x1_ref[0] = x
    hbuf[...] = _rmsnorm(x, gmix_ref[...]).astype(BF16)
    interleave(proj_u(0), iter(()))
    proj_xb()
    interleave(seq(proj_u(1), proj_u(2)),
               seq(conv(0), *[pool(q) for q in range(N_GRP)]))
    interleave(seq(proj_u(3), *[pool_proj(q) for q in range(N_GRP)], wout(pbuf, C_CONV)),
               seq(conv(1), conv(2)))
    interleave(wout(cbuf[0:2], 0), conv(3))
    interleave(wout(cbuf[2:4], 2 * GRP), iter(()))

    @pl.when(t == pl.num_programs(1) - 1)
    def _():
        for q in range(N_GRP):
            for jj in range(TILES_PER_GRP):
                lanes = _lane_tile(q * TILES_PER_GRP + jj)
                utail_ref[0, :, lanes] = ubuf[q][jj, T:T + CONV_HALO, :]
                xbtail_ref[0, :, lanes] = xbuf[q][jj, T:T + POOL_HALO, :]

    for q in range(N_GRP):
        ubuf[q][:, 0:CONV_HALO, :] = ubuf[q][:, T:T + CONV_HALO, :]
        xbuf[q][:, 0:POOL_HALO, :] = xbuf[q][:, T:T + POOL_HALO, :]


def _const_spec(shape):
    return pl.BlockSpec(shape, lambda *_: (0,) * len(shape), pipeline_mode=pl.Buffered(1))


def _mixer_weight_specs():
    return [
        _const_spec((1, D_MODEL)),
        _const_spec((N_GRP, D_MODEL, 2 * GRP)),
        _const_spec((D_MODEL, C_POOL)),
        _const_spec((CONV_WIDTH, C_CONV)),
        _const_spec((1, C_CONV)),
        _const_spec((1, C_CONV)),
        _const_spec((1, C_CONV)),
        _const_spec((N_GRP, GRP, GRP)),
        _const_spec((1, C_POOL)),
        _const_spec((C_CONV + C_POOL, D_MODEL)),
    ]


def _mixer_prompt(x_prompt, meta, mix_w):
    nb, seq, _ = x_prompt.shape
    T = T_MIX
    return pl.pallas_call(
        _mixer_prompt_kernel,
        grid=(nb, seq // T),
        in_specs=[
            pl.BlockSpec((1, T, D_MODEL), lambda b, t: (b, t, 0)),
            _const_spec((N_META, D_MODEL)),
        ] + _mixer_weight_specs(),
        out_specs=[
            pl.BlockSpec((1, T, D_MODEL), lambda b, t: (b, t, 0)),
            pl.BlockSpec((1, CONV_HALO, C_CONV), lambda b, t: (b, 0, 0)),
            pl.BlockSpec((1, POOL_HALO, C_POOL), lambda b, t: (b, 0, 0)),
        ],
        out_shape=[
            jax.ShapeDtypeStruct((nb, seq, D_MODEL), F32),
            jax.ShapeDtypeStruct((nb, CONV_HALO, C_CONV), F32),
            jax.ShapeDtypeStruct((nb, POOL_HALO, C_POOL), F32),
        ],
        scratch_shapes=[
            pltpu.VMEM((T, D_MODEL), BF16),
            pltpu.VMEM((N_GRP, TILES_PER_GRP, N_META, LANES), F32),
            pltpu.VMEM((N_GRP, TILES_PER_GRP, N_META, LANES), F32),
            pltpu.VMEM((CONV_WIDTH, SUBLANES, C_CONV), F32),
        ]
        + [pltpu.VMEM((TILES_PER_GRP, CONV_HALO + T, LANES), F32)] * N_GRP
        + [pltpu.VMEM((TILES_PER_GRP, POOL_HALO + T, LANES), F32)] * N_GRP
        + [pltpu.VMEM((T, GRP), BF16)] * N_GRP
        + [pltpu.VMEM((T, GRP), BF16)] * N_GRP
        + [pltpu.VMEM((T, GRP), BF16)] * N_GRP,
        compiler_params=pltpu.CompilerParams(
            dimension_semantics=("arbitrary", "arbitrary"), vmem_limit_bytes=VMEM_LIMIT),
        name="mixer_prompt",
    )(x_prompt, meta, *mix_w)


def _mixer_sample_kernel(xs_ref, hist_ref, phist_ref, gmix_ref, wag_ref, wxb_ref, wdw_ref, bdw_ref, lng_ref, lnb_ref,
                         wpool_ref, pscale_ref, wout_ref,
                         x1_ref, u_ref, xb_ref, cp):
    S = S_SMP
    nj = xs_ref.shape[0]
    x = jnp.concatenate([xs_ref[j] for j in range(nj)], axis=0)
    h = _rmsnorm(x, gmix_ref[...]).astype(BF16)
    for q in range(N_GRP):
        u = _glu(h, wag_ref[q])
        for j in range(nj):
            u_ref[j, :, q * GRP:(q + 1) * GRP] = u[j * S:(j + 1) * S]
    xb = jnp.dot(h, wxb_ref[...], preferred_element_type=F32)
    for j in range(nj):
        xb_ref[j] = xb[j * S:(j + 1) * S]

    def conv_tile(lt, carry):
        lane0 = pl.multiple_of(lt * LANES, LANES)
        lanes = pl.ds(lane0, LANES)
        bias = jnp.broadcast_to(bdw_ref[:, lanes], (S, LANES))
        acc = [bias for _ in range(nj)]
        for kk in range(CONV_HIST + nj):
            row = hist_ref[kk, :, lanes] if kk < CONV_HIST else u_ref[kk - CONV_HIST, :, lanes]
            for j in range(nj):
                k = kk - j
                if 0 <= k < CONV_WIDTH:
                    acc[j] = acc[j] + row * wdw_ref[k:k + 1, lanes]
        for j in range(nj):
            c = _head_ln_silu(acc[j], lng_ref[:, lanes], lnb_ref[:, lanes])
            cp[pl.ds(j * S, S), lanes] = c.astype(BF16)
        return carry

    lax.fori_loop(0, N_CTILE, conv_tile, 0)

    for gi, w in enumerate(POOL_WINDOWS):
        sl = slice(gi * GRP, (gi + 1) * GRP)
        rows = [phist_ref[i, :, sl] for i in range(POOL_HIST - w + 1, POOL_HIST)] + [xb_ref[j, :, sl] for j in range(nj)]
        ds = []
        for j in range(nj):
            win_rows = rows[j:j + w]
            s = win_rows[0]
            for r in win_rows[1:]:
                s = s + r
            ds.append(s * (1.0 / w) - win_rows[-1])
        d = jnp.concatenate(ds, axis=0).astype(BF16)
        p = jnp.dot(d, wpool_ref[gi], preferred_element_type=F32) * pscale_ref[:, sl]
        cp[:, C_CONV + gi * GRP:C_CONV + (gi + 1) * GRP] = p.astype(BF16)

    y = jnp.dot(cp[...], wout_ref[...], preferred_element_type=F32)
    x1 = x + y
    for j in range(nj):
        x1_ref[j] = x1[j * S:(j + 1) * S]


def _mixer_sample(xs_t, hist_t, phist_t, mix_w):
    nj, ns, _ = xs_t.shape
    S = S_SMP
    return pl.pallas_call(
        _mixer_sample_kernel,
        grid=(ns // S,),
        in_specs=[
            pl.BlockSpec((nj, S, D_MODEL), lambda s: (0, s, 0)),
            pl.BlockSpec((CONV_HIST, S, C_CONV), lambda s: (0, s, 0)),
            pl.BlockSpec((POOL_HIST, S, C_POOL), lambda s: (0, s, 0)),
        ] + _mixer_weight_specs(),
        out_specs=[
            pl.BlockSpec((nj, S, D_MODEL), lambda s: (0, s, 0)),
            pl.BlockSpec((nj, S, C_CONV), lambda s: (0, s, 0)),
            pl.BlockSpec((nj, S, C_POOL), lambda s: (0, s, 0)),
        ],
        out_shape=[
            jax.ShapeDtypeStruct((nj, ns, D_MODEL), F32),
            jax.ShapeDtypeStruct((nj, ns, C_CONV), F32),
            jax.ShapeDtypeStruct((nj, ns, C_POOL), F32),
        ],
        scratch_shapes=[pltpu.VMEM((nj * S, C_CONV + C_POOL), BF16)],
        compiler_params=pltpu.CompilerParams(
            dimension_semantics=("arbitrary",), vmem_limit_bytes=VMEM_LIMIT),
        name="mixer_sample",
    )(xs_t, hist_t, phist_t, *mix_w)


def _mlp_kernel(x_ref, gffn_ref, wup_ref, wdown_ref, gfin_ref, o_ref, h_ref):
    j = pl.program_id(1)

    @pl.when(j == 0)
    def _():
        x = x_ref[...]
        h_ref[...] = _rmsnorm(x, gffn_ref[...]).astype(BF16)
        o_ref[...] = x

    a = jnp.maximum(jnp.dot(h_ref[...], wup_ref[...].astype(BF16), preferred_element_type=F32), 0.0)
    o_ref[...] += jnp.dot((a * a).astype(BF16), wdown_ref[...].astype(BF16), preferred_element_type=F32)

    @pl.when(j == pl.num_programs(1) - 1)
    def _():
        o_ref[...] = _rmsnorm(o_ref[...], gfin_ref[...])


def _mlp(x1, gffn, wup, wdown, gfin):
    n, _ = x1.shape
    tm = min(TM_MLP, n)
    return pl.pallas_call(
        _mlp_kernel,
        grid=(n // tm, D_FF // TF_MLP),
        in_specs=[
            pl.BlockSpec((tm, D_MODEL), lambda i, j: (i, 0), pipeline_mode=pl.Buffered(1)),
            _const_spec((1, D_MODEL)),
            pl.BlockSpec((D_MODEL, TF_MLP), lambda i, j: (0, j)),
            pl.BlockSpec((TF_MLP, D_MODEL), lambda i, j: (j, 0)),
            _const_spec((1, D_MODEL)),
        ],
        out_specs=pl.BlockSpec((tm, D_MODEL), lambda i, j: (i, 0)),
        out_shape=jax.ShapeDtypeStruct((n, D_MODEL), F32),
        scratch_shapes=[pltpu.VMEM((tm, D_MODEL), BF16)],
        compiler_params=pltpu.CompilerParams(
            dimension_semantics=("arbitrary", "arbitrary"), vmem_limit_bytes=VMEM_LIMIT),
        name="mlp",
    )(x1, gffn, wup, wdown, gfin)


def kernel(x_prompt, x_sample, state_conv, state_pool, meta_tokens, norm_mix_g, w_in, w_dw, b_dw, conv_ln_g, conv_ln_b, w_pool, pool_scale, w_out, norm_ffn_g, w_up, w_down, final_norm_g):
    assert norm_mix_g.shape[0] == 1, "single-layer step"
    nb, seq, _ = x_prompt.shape
    ns, nj, _ = x_sample.shape

    gmix = norm_mix_g[0][None]
    gffn = norm_ffn_g[0][None]
    gfin = final_norm_g[None]
    wag = w_in[0][:, :2 * C_CONV].reshape(D_MODEL, 2, N_GRP, GRP).transpose(2, 0, 1, 3).reshape(N_GRP, D_MODEL, 2 * GRP)
    mix_w = (gmix, wag.astype(BF16), w_in[0][:, 2 * C_CONV:].astype(BF16), w_dw[0], b_dw[0][None],
             conv_ln_g[0][None], conv_ln_b[0][None], w_pool[0].astype(BF16), pool_scale[0][None], w_out[0].astype(BF16))

    x1p, utail, xbtail = _mixer_prompt(x_prompt, meta_tokens, mix_w)
    y_prompt = _mlp(x1p.reshape(nb * seq, D_MODEL), gffn, w_up[0], w_down[0], gfin).reshape(nb, seq, D_MODEL)
    new_conv_prompt = utail[:, CONV_HALO - CONV_HIST:][None]
    new_pool_prompt = xbtail[:, POOL_HALO - POOL_HIST:][None]

    xs_t = jnp.transpose(x_sample, (1, 0, 2))
    hist_t = jnp.transpose(state_conv[0], (1, 0, 2))
    phist_t = jnp.transpose(state_pool[0], (1, 0, 2))
    x1s_t, u_t, xb_t = _mixer_sample(xs_t, hist_t, phist_t, mix_w)
    ys_t = _mlp(x1s_t.reshape(nj * ns, D_MODEL), gffn, w_up[0], w_down[0], gfin).reshape(nj, ns, D_MODEL)
    y_sample = jnp.transpose(ys_t, (1, 0, 2))
    new_conv_sample = jnp.concatenate([state_conv[0][:, nj:], jnp.transpose(u_t, (1, 0, 2))], axis=1)[None]
    new_pool_sample = jnp.concatenate([state_pool[0][:, nj:], jnp.transpose(xb_t, (1, 0, 2))], axis=1)[None]
    return (y_prompt, y_sample, new_conv_prompt, new_pool_prompt, new_conv_sample, new_pool_sample)
```

```python
import jax
import jax.numpy as jnp
from jax import lax
from jax.experimental import pallas as pl
from jax.experimental.pallas import tpu as pltpu

D_MODEL = 2048
C_CONV = 1024
C_POOL = 1024
CONV_WIDTH = 31
CONV_HIST = CONV_WIDTH - 1
POOL_WINDOWS = (2, 4, 8, 16)
N_GRP = len(POOL_WINDOWS)
GRP = C_POOL // N_GRP
POOL_HIST = max(POOL_WINDOWS) - 1
D_FF = 4 * D_MODEL
N_META = 16
EPS = 1e-6

LANES = 128
SUBLANES = 8
N_CTILE = C_CONV // LANES
TILES_PER_GRP = GRP // LANES
CONV_HALO = 32
POOL_HALO = 16

T_MIX = 256
R_CONV = 64
KC_MIX = 256
S_SMP = 32
TM_MLP = 1024
TF_MLP = 512
VMEM_LIMIT = 56 * 1024 * 1024

F32 = jnp.float32
BF16 = jnp.bfloat16


def _rmsnorm(x, g):
    return x * lax.rsqrt(jnp.mean(x * x, axis=-1, keepdims=True) + EPS) * g


def _glu(h, wag_q):
    proj = jnp.dot(h, wag_q, preferred_element_type=F32)
    return proj[:, :GRP] * jax.nn.sigmoid(proj[:, GRP:])


def _head_ln_silu(c, g, b):
    mu = jnp.mean(c, axis=-1, keepdims=True)
    d = c - mu
    var = jnp.mean(d * d, axis=-1, keepdims=True)
    y = d * lax.rsqrt(var + EPS) * g + b
    return y * jax.nn.sigmoid(y)


def _lane_tile(j):
    return slice(j * LANES, (j + 1) * LANES)


def _mixer_prompt_kernel(x_ref, meta_ref, gmix_ref, win_ref, wdw_ref, bdw_ref, lng_ref, lnb_ref,
                         wpool_ref, pscale_ref, wout_ref,
                         x1_ref, utail_ref, xbtail_ref,
                         hbuf, umeta, xbmeta, wb, wag_ref, *grp_scratch):
    b = pl.program_id(0)
    t = pl.program_id(1)
    T = T_MIX
    wxb_ref = win_ref.at[:, 2 * C_CONV:]
    ubuf, xbuf, dbuf, cbuf, pbuf = (grp_scratch[i * N_GRP:(i + 1) * N_GRP] for i in range(5))

    def put(dst, v, row0):
        for jj in range(TILES_PER_GRP):
            dst[jj, row0:row0 + v.shape[0], :] = v[:, _lane_tile(jj)]

    @pl.when((b == 0) & (t == 0))
    def _():
        for q in range(N_GRP):
            wag_ref[q, :, 0:GRP] = win_ref[:, q * GRP:(q + 1) * GRP]
            wag_ref[q, :, GRP:2 * GRP] = win_ref[:, C_CONV + q * GRP:C_CONV + (q + 1) * GRP]
        hm = _rmsnorm(meta_ref[...], gmix_ref[...]).astype(BF16)
        xm = jnp.dot(hm, wxb_ref[...], preferred_element_type=F32)
        for q in range(N_GRP):
            put(umeta.at[q], _glu(hm, wag_ref[q]), 0)
            put(xbmeta.at[q], xm[:, q * GRP:(q + 1) * GRP], 0)
        for k in range(CONV_WIDTH):
            wb[k] = jnp.broadcast_to(wdw_ref[k:k + 1, :], (SUBLANES, C_CONV))

    @pl.when(t == 0)
    def _():
        for q in range(N_GRP):
            ubuf[q][:, 0:CONV_HALO - N_META, :] = jnp.zeros((TILES_PER_GRP, CONV_HALO - N_META, LANES), F32)
            ubuf[q][:, CONV_HALO - N_META:CONV_HALO, :] = umeta[q]
            xbuf[q][:, 0:POOL_HALO, :] = xbmeta[q]

    def proj_xb():
        xb = jnp.dot(hbuf[...], wxb_ref[...], preferred_element_type=F32)
        for q in range(N_GRP):
            put(xbuf[q], xb[:, q * GRP:(q + 1) * GRP], POOL_HALO)

    def proj_u(q):
        acc = None
        for k0 in range(0, D_MODEL, KC_MIX):
            d = jnp.dot(hbuf[:, k0:k0 + KC_MIX], wag_ref[q, k0:k0 + KC_MIX, :], preferred_element_type=F32)
            acc = d if acc is None else acc + d
            yield
        put(ubuf[q], acc[:, :GRP] * jax.nn.sigmoid(acc[:, GRP:]), CONV_HALO)
        yield

    def conv(q):
        for jj in range(TILES_PER_GRP):
            lanes = _lane_tile(q * TILES_PER_GRP + jj)
            for r0 in range(0, T, R_CONV):
                acc = jnp.broadcast_to(bdw_ref[:, lanes], (R_CONV, LANES))
                for k in range(CONV_WIDTH):
                    row = r0 + (CONV_HALO - CONV_HIST) + k
                    acc = acc + ubuf[q][jj, row:row + R_CONV, :] * jnp.tile(wb[k, :, lanes], (R_CONV // SUBLANES, 1))
                c = _head_ln_silu(acc, lng_ref[:, lanes], lnb_ref[:, lanes])
                cbuf[q][r0:r0 + R_CONV, _lane_tile(jj)] = c.astype(BF16)
                yield

    def pool(q):
        w = POOL_WINDOWS[q]
        for jj in range(TILES_PER_GRP):
            for r0 in range(0, T, R_CONV):
                cur = xbuf[q][jj, POOL_HALO + r0:POOL_HALO + r0 + R_CONV, :]
                s = cur
                for d in range(1, w):
                    s = s + xbuf[q][jj, POOL_HALO + r0 - d:POOL_HALO + r0 - d + R_CONV, :]
                dbuf[q][r0:r0 + R_CONV, _lane_tile(jj)] = (s * (1.0 / w) - cur).astype(BF16)
            yield

    def pool_proj(q):
        p = jnp.dot(dbuf[q][...], wpool_ref[q], preferred_element_type=F32) * pscale_ref[:, q * GRP:(q + 1) * GRP]
        pbuf[q][...] = p.astype(BF16)
        yield

    def wout(bufs, k0):
        acc = x1_ref[0]
        for i, buf in enumerate(bufs):
            acc = acc + jnp.dot(buf[...], wout_ref[k0 + i * GRP:k0 + (i + 1) * GRP, :], preferred_element_type=F32)
            yield
        x1_ref[0] = acc
        yield

    def seq(*gens):
        for g in gens:
            yield from g

    def interleave(a, b):
        live = [a, b]
        while live:
            for g in list(live):
                if next(g, StopIteration) is StopIteration:
                    live.remove(g)

    x = x_ref[0]
    x1_ref[0] = x
    hbuf[...] = _rmsnorm(x, gmix_ref[...]).astype(BF16)
    interleave(proj_u(0), iter(()))
    proj_xb()
    interleave(seq(proj_u(1), proj_u(2)),
               seq(conv(0), *[pool(q) for q in range(N_GRP)]))
    interleave(seq(proj_u(3), *[pool_proj(q) for q in range(N_GRP)], wout(pbuf, C_CONV)),
               seq(conv(1), conv(2)))
    interleave(wout(cbuf[0:2], 0), conv(3))
    interleave(wout(cbuf[2:4], 2 * GRP), iter(()))

    @pl.when(t == pl.num_programs(1) - 1)
    def _():
        for q in range(N_GRP):
            for jj in range(TILES_PER_GRP):
                lanes = _lane_tile(q * TILES_PER_GRP + jj)
                utail_ref[0, :, lanes] = ubuf[q][jj, T:T + CONV_HALO, :]
                xbtail_ref[0, :, lanes] = xbuf[q][jj, T:T + POOL_HALO, :]

    for q in range(N_GRP):
        ubuf[q][:, 0:CONV_HALO, :] = ubuf[q][:, T:T + CONV_HALO, :]
        xbuf[q][:, 0:POOL_HALO, :] = xbuf[q][:, T:T + POOL_HALO, :]


def _const_spec(shape):
    return pl.BlockSpec(shape, lambda *_: (0,) * len(shape), pipeline_mode=pl.Buffered(1))


def _mixer_weight_specs():
    return [
        _const_spec((1, D_MODEL)),
        _const_spec((D_MODEL, 2 * C_CONV + C_POOL)),
        _const_spec((CONV_WIDTH, C_CONV)),
        _const_spec((1, C_CONV)),
        _const_spec((1, C_CONV)),
        _const_spec((1, C_CONV)),
        _const_spec((N_GRP, GRP, GRP)),
        _const_spec((1, C_POOL)),
        _const_spec((C_CONV + C_POOL, D_MODEL)),
    ]


def _mixer_prompt(x_prompt, meta, mix_w):
    nb, seq, _ = x_prompt.shape
    T = T_MIX
    return pl.pallas_call(
        _mixer_prompt_kernel,
        grid=(nb, seq // T),
        in_specs=[
            pl.BlockSpec((1, T, D_MODEL), lambda b, t: (b, t, 0)),
            _const_spec((N_META, D_MODEL)),
        ] + _mixer_weight_specs(),
        out_specs=[
            pl.BlockSpec((1, T, D_MODEL), lambda b, t: (b, t, 0)),
            pl.BlockSpec((1, CONV_HALO, C_CONV), lambda b, t: (b, 0, 0)),
            pl.BlockSpec((1, POOL_HALO, C_POOL), lambda b, t: (b, 0, 0)),
        ],
        out_shape=[
            jax.ShapeDtypeStruct((nb, seq, D_MODEL), F32),
            jax.ShapeDtypeStruct((nb, CONV_HALO, C_CONV), F32),
            jax.ShapeDtypeStruct((nb, POOL_HALO, C_POOL), F32),
        ],
        scratch_shapes=[
            pltpu.VMEM((T, D_MODEL), BF16),
            pltpu.VMEM((N_GRP, TILES_PER_GRP, N_META, LANES), F32),
            pltpu.VMEM((N_GRP, TILES_PER_GRP, N_META, LANES), F32),
            pltpu.VMEM((CONV_WIDTH, SUBLANES, C_CONV), F32),
            pltpu.VMEM((N_GRP, D_MODEL, 2 * GRP), BF16),
        ]
        + [pltpu.VMEM((TILES_PER_GRP, CONV_HALO + T, LANES), F32)] * N_GRP
        + [pltpu.VMEM((TILES_PER_GRP, POOL_HALO + T, LANES), F32)] * N_GRP
        + [pltpu.VMEM((T, GRP), BF16)] * N_GRP
        + [pltpu.VMEM((T, GRP), BF16)] * N_GRP
        + [pltpu.VMEM((T, GRP), BF16)] * N_GRP,
        compiler_params=pltpu.CompilerParams(
            dimension_semantics=("arbitrary", "arbitrary"), vmem_limit_bytes=VMEM_LIMIT),
        name="mixer_prompt",
    )(x_prompt, meta, *mix_w)


def _mixer_sample_kernel(xs_ref, hist_ref, phist_ref, gmix_ref, win_ref, wdw_ref, bdw_ref, lng_ref, lnb_ref,
                         wpool_ref, pscale_ref, wout_ref,
                         x1_ref, newc_ref, newp_ref, cp):
    S = S_SMP
    nj = xs_ref.shape[0]
    n_keep_c = CONV_HIST - nj
    n_keep_p = POOL_HIST - nj
    x = jnp.concatenate([xs_ref[j] for j in range(nj)], axis=0)
    h = _rmsnorm(x, gmix_ref[...]).astype(BF16)
    newc_ref[0:n_keep_c] = hist_ref[nj:CONV_HIST]
    newp_ref[0:n_keep_p] = phist_ref[nj:POOL_HIST]
    proj = jnp.dot(h, win_ref[...], preferred_element_type=F32)
    u = proj[:, :C_CONV] * jax.nn.sigmoid(proj[:, C_CONV:2 * C_CONV])
    for j in range(nj):
        newc_ref[n_keep_c + j] = u[j * S:(j + 1) * S]
        newp_ref[n_keep_p + j] = proj[j * S:(j + 1) * S, 2 * C_CONV:]

    def conv_tile(lt, carry):
        lane0 = pl.multiple_of(lt * LANES, LANES)
        lanes = pl.ds(lane0, LANES)
        bias = jnp.broadcast_to(bdw_ref[:, lanes], (S, LANES))
        acc = [bias for _ in range(nj)]
        for kk in range(CONV_HIST + nj):
            row = hist_ref[kk, :, lanes] if kk < CONV_HIST else newc_ref[kk - nj, :, lanes]
            for j in range(nj):
                k = kk - j
                if 0 <= k < CONV_WIDTH:
                    acc[j] = acc[j] + row * wdw_ref[k:k + 1, lanes]
        for j in range(nj):
            c = _head_ln_silu(acc[j], lng_ref[:, lanes], lnb_ref[:, lanes])
            cp[pl.ds(j * S, S), lanes] = c.astype(BF16)
        return carry

    lax.fori_loop(0, N_CTILE, conv_tile, 0)

    for gi, w in enumerate(POOL_WINDOWS):
        sl = slice(gi * GRP, (gi + 1) * GRP)
        rows = ([phist_ref[i, :, sl] for i in range(POOL_HIST - w + 1, POOL_HIST)]
                + [newp_ref[n_keep_p + j, :, sl] for j in range(nj)])
        ds = []
        for j in range(nj):
            win_rows = rows[j:j + w]
            s = win_rows[0]
            for r in win_rows[1:]:
                s = s + r
            ds.append(s * (1.0 / w) - win_rows[-1])
        d = jnp.concatenate(ds, axis=0).astype(BF16)
        p = jnp.dot(d, wpool_ref[gi], preferred_element_type=F32) * pscale_ref[:, sl]
        cp[:, C_CONV + gi * GRP:C_CONV + (gi + 1) * GRP] = p.astype(BF16)

    y = jnp.dot(cp[...], wout_ref[...], preferred_element_type=F32)
    x1 = x + y
    for j in range(nj):
        x1_ref[j] = x1[j * S:(j + 1) * S]


def _mixer_sample(xs_t, hist, phist, mix_w):
    nj, ns, _ = xs_t.shape
    S = S_SMP
    return pl.pallas_call(
        _mixer_sample_kernel,
        grid=(ns // S,),
        in_specs=[
            pl.BlockSpec((nj, S, D_MODEL), lambda s: (0, s, 0)),
            pl.BlockSpec((CONV_HIST, S, C_CONV), lambda s: (0, s, 0)),
            pl.BlockSpec((POOL_HIST, S, C_POOL), lambda s: (0, s, 0)),
        ] + _mixer_weight_specs(),
        out_specs=[
            pl.BlockSpec((nj, S, D_MODEL), lambda s: (0, s, 0)),
            pl.BlockSpec((CONV_HIST, S, C_CONV), lambda s: (0, s, 0)),
            pl.BlockSpec((POOL_HIST, S, C_POOL), lambda s: (0, s, 0)),
        ],
        out_shape=[
            jax.ShapeDtypeStruct((nj, ns, D_MODEL), F32),
            jax.ShapeDtypeStruct((CONV_HIST, ns, C_CONV), F32),
            jax.ShapeDtypeStruct((POOL_HIST, ns, C_POOL), F32),
        ],
        scratch_shapes=[pltpu.VMEM((nj * S, C_CONV + C_POOL), BF16)],
        compiler_params=pltpu.CompilerParams(
            dimension_semantics=("arbitrary",), vmem_limit_bytes=VMEM_LIMIT),
        name="mixer_sample",
    )(xs_t, hist, phist, *mix_w)


def _mlp_kernel(x_ref, gffn_ref, wup_ref, wdown_ref, gfin_ref, o_ref, h_ref):
    j = pl.program_id(1)

    @pl.when(j == 0)
    def _():
        x = x_ref[...]
        h_ref[...] = _rmsnorm(x, gffn_ref[...]).astype(BF16)
        o_ref[...] = x

    a = jnp.maximum(jnp.dot(h_ref[...], wup_ref[...].astype(BF16), preferred_element_type=F32), 0.0)
    o_ref[...] += jnp.dot((a * a).astype(BF16), wdown_ref[...].astype(BF16), preferred_element_type=F32)

    @pl.when(j == pl.num_programs(1) - 1)
    def _():
        o_ref[...] = _rmsnorm(o_ref[...], gfin_ref[...])


def _mlp(x1, gffn, wup, wdown, gfin):
    n, _ = x1.shape
    tm = min(TM_MLP, n)
    return pl.pallas_call(
        _mlp_kernel,
        grid=(n // tm, D_FF // TF_MLP),
        in_specs=[
            pl.BlockSpec((tm, D_MODEL), lambda i, j: (i, 0), pipeline_mode=pl.Buffered(1)),
            _const_spec((1, D_MODEL)),
            pl.BlockSpec((D_MODEL, TF_MLP), lambda i, j: (0, j)),
            pl.BlockSpec((TF_MLP, D_MODEL), lambda i, j: (j, 0)),
            _const_spec((1, D_MODEL)),
        ],
        out_specs=pl.BlockSpec((tm, D_MODEL), lambda i, j: (i, 0)),
        out_shape=jax.ShapeDtypeStruct((n, D_MODEL), F32),
        scratch_shapes=[pltpu.VMEM((tm, D_MODEL), BF16)],
        compiler_params=pltpu.CompilerParams(
            dimension_semantics=("arbitrary", "arbitrary"), vmem_limit_bytes=VMEM_LIMIT),
        name="mlp",
    )(x1, gffn, wup, wdown, gfin)


def kernel(x_prompt, x_sample, state_conv, state_pool, meta_tokens, norm_mix_g, w_in, w_dw, b_dw, conv_ln_g, conv_ln_b, w_pool, pool_scale, w_out, norm_ffn_g, w_up, w_down, final_norm_g):
    assert norm_mix_g.shape[0] == 1, "single-layer step"
    nb, seq, _ = x_prompt.shape
    ns, nj, _ = x_sample.shape

    gmix = norm_mix_g[0][None]
    gffn = norm_ffn_g[0][None]
    gfin = final_norm_g[None]
    mix_w = (gmix, w_in[0].astype(BF16), w_dw[0], b_dw[0][None],
             conv_ln_g[0][None], conv_ln_b[0][None], w_pool[0].astype(BF16), pool_scale[0][None], w_out[0].astype(BF16))

    x1p, utail, xbtail = _mixer_prompt(x_prompt, meta_tokens, mix_w)
    y_prompt = _mlp(x1p.reshape(nb * seq, D_MODEL), gffn, w_up[0], w_down[0], gfin).reshape(nb, seq, D_MODEL)
    new_conv_prompt = utail[:, CONV_HALO - CONV_HIST:][None]
    new_pool_prompt = xbtail[:, POOL_HALO - POOL_HIST:][None]

    xs_t = jnp.transpose(x_sample, (1, 0, 2))
    hist_t = jnp.transpose(state_conv[0], (1, 0, 2))
    phist_t = jnp.transpose(state_pool[0], (1, 0, 2))
    x1s_t, newc_t, newp_t = _mixer_sample(xs_t, hist_t, phist_t, mix_w)
    ys_t = _mlp(x1s_t.reshape(nj * ns, D_MODEL), gffn, w_up[0], w_down[0], gfin).reshape(nj, ns, D_MODEL)
    y_sample = jnp.transpose(ys_t, (1, 0, 2))
    new_conv_sample = jnp.transpose(newc_t, (1, 0, 2))[None]
    new_pool_sample = jnp.transpose(newp_t, (1, 0, 2))[None]
    return (y_prompt, y_sample, new_conv_prompt, new_pool_prompt, new_conv_sample, new_pool_sample)
```

```python
import jax
import jax.numpy as jnp
from jax import lax
from jax.experimental import pallas as pl
from jax.experimental.pallas import tpu as pltpu

D_MODEL = 2048
C_CONV = 1024
C_POOL = 1024
CONV_WIDTH = 31
CONV_HIST = CONV_WIDTH - 1
POOL_WINDOWS = (2, 4, 8, 16)
N_GRP = len(POOL_WINDOWS)
GRP = C_POOL // N_GRP
POOL_HIST = max(POOL_WINDOWS) - 1
D_FF = 4 * D_MODEL
N_META = 16
EPS = 1e-6

LANES = 128
SUBLANES = 8
N_CTILE = C_CONV // LANES
TILES_PER_GRP = GRP // LANES
CONV_HALO = 32
POOL_HALO = 16

T_MIX = 256
R_CONV = 64
KC_MIX = 256
S_SMP = 32
TM_MLP = 1024
TF_MLP = 512
VMEM_LIMIT = 56 * 1024 * 1024

F32 = jnp.float32
BF16 = jnp.bfloat16


def _rmsnorm(x, g):
    return x * lax.rsqrt(jnp.mean(x * x, axis=-1, keepdims=True) + EPS) * g


def _glu(h, wag_q):
    proj = jnp.dot(h, wag_q, preferred_element_type=F32)
    return proj[:, :GRP] * jax.nn.sigmoid(proj[:, GRP:])


def _head_ln_silu(c, g, b):
    mu = jnp.mean(c, axis=-1, keepdims=True)
    d = c - mu
    var = jnp.mean(d * d, axis=-1, keepdims=True)
    y = d * lax.rsqrt(var + EPS) * g + b
    return y * jax.nn.sigmoid(y)


def _lane_tile(j):
    return slice(j * LANES, (j + 1) * LANES)


def _mixer_prompt_kernel(x_ref, meta_ref, gmix_ref, win_ref, wdw_ref, bdw_ref, lng_ref, lnb_ref,
                         wpool_ref, pscale_ref, wout_ref,
                         x1_ref, utail_ref, xbtail_ref,
                         hbuf, umeta, xbmeta, wb, wag_ref, *grp_scratch):
    b = pl.program_id(0)
    t = pl.program_id(1)
    T = T_MIX
    wxb_ref = win_ref.at[:, 2 * C_CONV:]
    ubuf, xbuf, dbuf, cbuf, pbuf = (grp_scratch[i * N_GRP:(i + 1) * N_GRP] for i in range(5))

    def put(dst, v, row0):
        for jj in range(TILES_PER_GRP):
            dst[jj, row0:row0 + v.shape[0], :] = v[:, _lane_tile(jj)]

    @pl.when((b == 0) & (t == 0))
    def _():
        for q in range(N_GRP):
            wag_ref[q, :, 0:GRP] = win_ref[:, q * GRP:(q + 1) * GRP]
            wag_ref[q, :, GRP:2 * GRP] = win_ref[:, C_CONV + q * GRP:C_CONV + (q + 1) * GRP]
        hm = _rmsnorm(meta_ref[...], gmix_ref[...]).astype(BF16)
        xm = jnp.dot(hm, wxb_ref[...], preferred_element_type=F32)
        for q in range(N_GRP):
            put(umeta.at[q], _glu(hm, wag_ref[q]), 0)
            put(xbmeta.at[q], xm[:, q * GRP:(q + 1) * GRP], 0)
        for k in range(CONV_WIDTH):
            wb[k] = jnp.broadcast_to(wdw_ref[k:k + 1, :], (SUBLANES, C_CONV))

    @pl.when(t == 0)
    def _():
        for q in range(N_GRP):
            ubuf[q][:, 0:CONV_HALO - N_META, :] = jnp.zeros((TILES_PER_GRP, CONV_HALO - N_META, LANES), F32)
            ubuf[q][:, CONV_HALO - N_META:CONV_HALO, :] = umeta[q]
            xbuf[q][:, 0:POOL_HALO, :] = xbmeta[q]

    def proj_xb():
        xb = jnp.dot(hbuf[...], wxb_ref[...], preferred_element_type=F32)
        for q in range(N_GRP):
            put(xbuf[q], xb[:, q * GRP:(q + 1) * GRP], POOL_HALO)

    def proj_u(q):
        acc = None
        for k0 in range(0, D_MODEL, KC_MIX):
            d = jnp.dot(hbuf[:, k0:k0 + KC_MIX], wag_ref[q, k0:k0 + KC_MIX, :], preferred_element_type=F32)
            acc = d if acc is None else acc + d
            yield
        put(ubuf[q], acc[:, :GRP] * jax.nn.sigmoid(acc[:, GRP:]), CONV_HALO)
        yield

    def conv(q):
        for jj in range(TILES_PER_GRP):
            lanes = _lane_tile(q * TILES_PER_GRP + jj)
            for r0 in range(0, T, R_CONV):
                acc = jnp.broadcast_to(bdw_ref[:, lanes], (R_CONV, LANES))
                for k in range(CONV_WIDTH):
                    row = r0 + (CONV_HALO - CONV_HIST) + k
                    acc = acc + ubuf[q][jj, row:row + R_CONV, :] * jnp.tile(wb[k, :, lanes], (R_CONV // SUBLANES, 1))
                c = _head_ln_silu(acc, lng_ref[:, lanes], lnb_ref[:, lanes])
                cbuf[q][r0:r0 + R_CONV, _lane_tile(jj)] = c.astype(BF16)
                yield

    def pool(q):
        w = POOL_WINDOWS[q]
        for jj in range(TILES_PER_GRP):
            for r0 in range(0, T, R_CONV):
                cur = xbuf[q][jj, POOL_HALO + r0:POOL_HALO + r0 + R_CONV, :]
                s = cur
                for d in range(1, w):
                    s = s + xbuf[q][jj, POOL_HALO + r0 - d:POOL_HALO + r0 - d + R_CONV, :]
                dbuf[q][r0:r0 + R_CONV, _lane_tile(jj)] = (s * (1.0 / w) - cur).astype(BF16)
            yield

    def pool_proj(q):
        p = jnp.dot(dbuf[q][...], wpool_ref[q], preferred_element_type=F32) * pscale_ref[:, q * GRP:(q + 1) * GRP]
        pbuf[q][...] = p.astype(BF16)
        yield

    def wout(bufs, k0):
        acc = x1_ref[0]
        for i, buf in enumerate(bufs):
            acc = acc + jnp.dot(buf[...], wout_ref[k0 + i * GRP:k0 + (i + 1) * GRP, :], preferred_element_type=F32)
            yield
        x1_ref[0] = acc
        yield

    def seq(*gens):
        for g in gens:
            yield from g

    def interleave(a, b):
        live = [a, b]
        while live:
            for g in list(live):
                if next(g, StopIteration) is StopIteration:
                    live.remove(g)

    x = x_ref[0]
    x1_ref[0] = x
    hbuf[...] = _rmsnorm(x, gmix_ref[...]).astype(BF16)
    interleave(proj_u(0), iter(()))
    proj_xb()
    interleave(seq(proj_u(1), proj_u(2)),
               seq(conv(0), *[pool(q) for q in range(N_GRP)]))
    interleave(seq(proj_u(3), *[pool_proj(q) for q in range(N_GRP)], wout(pbuf, C_CONV)),
               seq(conv(1), conv(2)))
    interleave(wout(cbuf[0:2], 0), conv(3))
    interleave(wout(cbuf[2:4], 2 * GRP), iter(()))

    @pl.when(t == pl.num_programs(1) - 1)
    def _():
        for q in range(N_GRP):
            for jj in range(TILES_PER_GRP):
                lanes = _lane_tile(q * TILES_PER_GRP + jj)
                utail_ref[0, :, lanes] = ubuf[q][jj, T:T + CONV_HALO, :]
                xbtail_ref[0, :, lanes] = xbuf[q][jj, T:T + POOL_HALO, :]

    for q in range(N_GRP):
        ubuf[q][:, 0:CONV_HALO, :] = ubuf[q][:, T:T + CONV_HALO, :]
        xbuf[q][:, 0:POOL_HALO, :] = xbuf[q][:, T:T + POOL_HALO, :]


def _const_spec(shape):
    return pl.BlockSpec(shape, lambda *_: (0,) * len(shape), pipeline_mode=pl.Buffered(1))


def _mixer_weight_specs():
    return [
        _const_spec((1, D_MODEL)),
        _const_spec((D_MODEL, 2 * C_CONV + C_POOL)),
        _const_spec((CONV_WIDTH, C_CONV)),
        _const_spec((1, C_CONV)),
        _const_spec((1, C_CONV)),
        _const_spec((1, C_CONV)),
        _const_spec((N_GRP, GRP, GRP)),
        _const_spec((1, C_POOL)),
        _const_spec((C_CONV + C_POOL, D_MODEL)),
    ]


def _mixer_prompt(x_prompt, meta, mix_w):
    nb, seq, _ = x_prompt.shape
    T = T_MIX
    return pl.pallas_call(
        _mixer_prompt_kernel,
        grid=(nb, seq // T),
        in_specs=[
            pl.BlockSpec((1, T, D_MODEL), lambda b, t: (b, t, 0)),
            _const_spec((N_META, D_MODEL)),
        ] + _mixer_weight_specs(),
        out_specs=[
            pl.BlockSpec((1, T, D_MODEL), lambda b, t: (b, t, 0)),
            pl.BlockSpec((1, CONV_HALO, C_CONV), lambda b, t: (b, 0, 0)),
            pl.BlockSpec((1, POOL_HALO, C_POOL), lambda b, t: (b, 0, 0)),
        ],
        out_shape=[
            jax.ShapeDtypeStruct((nb, seq, D_MODEL), F32),
            jax.ShapeDtypeStruct((nb, CONV_HALO, C_CONV), F32),
            jax.ShapeDtypeStruct((nb, POOL_HALO, C_POOL), F32),
        ],
        scratch_shapes=[
            pltpu.VMEM((T, D_MODEL), BF16),
            pltpu.VMEM((N_GRP, TILES_PER_GRP, N_META, LANES), F32),
            pltpu.VMEM((N_GRP, TILES_PER_GRP, N_META, LANES), F32),
            pltpu.VMEM((CONV_WIDTH, SUBLANES, C_CONV), F32),
            pltpu.VMEM((N_GRP, D_MODEL, 2 * GRP), BF16),
        ]
        + [pltpu.VMEM((TILES_PER_GRP, CONV_HALO + T, LANES), F32)] * N_GRP
        + [pltpu.VMEM((TILES_PER_GRP, POOL_HALO + T, LANES), F32)] * N_GRP
        + [pltpu.VMEM((T, GRP), BF16)] * N_GRP
        + [pltpu.VMEM((T, GRP), BF16)] * N_GRP
        + [pltpu.VMEM((T, GRP), BF16)] * N_GRP,
        compiler_params=pltpu.CompilerParams(
            dimension_semantics=("arbitrary", "arbitrary"), vmem_limit_bytes=VMEM_LIMIT),
        name="mixer_prompt",
    )(x_prompt, meta, *mix_w)


def _mixer_sample_kernel(xs_ref, hist_ref, phist_ref, gmix_ref, win_ref, wdw_ref, bdw_ref, lng_ref, lnb_ref,
                         wpool_ref, pscale_ref, wout_ref,
                         x1_ref, newc_ref, newp_ref, cp):
    S = S_SMP
    nj = xs_ref.shape[0]
    n_keep_c = CONV_HIST - nj
    n_keep_p = POOL_HIST - nj
    x = jnp.concatenate([xs_ref[j] for j in range(nj)], axis=0)
    h = _rmsnorm(x, gmix_ref[...]).astype(BF16)
    newc_ref[0:n_keep_c] = hist_ref[nj:CONV_HIST]
    newp_ref[0:n_keep_p] = phist_ref[nj:POOL_HIST]
    proj = jnp.dot(h, win_ref[...], preferred_element_type=F32)
    u = proj[:, :C_CONV] * jax.nn.sigmoid(proj[:, C_CONV:2 * C_CONV])
    for j in range(nj):
        newc_ref[n_keep_c + j] = u[j * S:(j + 1) * S]
        newp_ref[n_keep_p + j] = proj[j * S:(j + 1) * S, 2 * C_CONV:]

    def conv_tile(lt, carry):
        lane0 = pl.multiple_of(lt * LANES, LANES)
        lanes = pl.ds(lane0, LANES)
        bias = jnp.broadcast_to(bdw_ref[:, lanes], (S, LANES))
        acc = [bias for _ in range(nj)]
        for kk in range(CONV_HIST + nj):
            row = hist_ref[kk, :, lanes] if kk < CONV_HIST else newc_ref[kk - nj, :, lanes]
            for j in range(nj):
                k = kk - j
                if 0 <= k < CONV_WIDTH:
                    acc[j] = acc[j] + row * wdw_ref[k:k + 1, lanes]
        for j in range(nj):
            c = _head_ln_silu(acc[j], lng_ref[:, lanes], lnb_ref[:, lanes])
            cp[pl.ds(j * S, S), lanes] = c.astype(BF16)
        return carry

    lax.fori_loop(0, N_CTILE, conv_tile, 0)

    for gi, w in enumerate(POOL_WINDOWS):
        sl = slice(gi * GRP, (gi + 1) * GRP)
        rows = ([phist_ref[i, :, sl] for i in range(POOL_HIST - w + 1, POOL_HIST)]
                + [newp_ref[n_keep_p + j, :, sl] for j in range(nj)])
        ds = []
        for j in range(nj):
            win_rows = rows[j:j + w]
            s = win_rows[0]
            for r in win_rows[1:]:
                s = s + r
            ds.append(s * (1.0 / w) - win_rows[-1])
        d = jnp.concatenate(ds, axis=0).astype(BF16)
        p = jnp.dot(d, wpool_ref[gi], preferred_element_type=F32) * pscale_ref[:, sl]
        cp[:, C_CONV + gi * GRP:C_CONV + (gi + 1) * GRP] = p.astype(BF16)

    y = jnp.dot(cp[...], wout_ref[...], preferred_element_type=F32)
    x1 = x + y
    for j in range(nj):
        x1_ref[j] = x1[j * S:(j + 1) * S]


def _mixer_sample(xs_t, hist, phist, mix_w):
    nj, ns, _ = xs_t.shape
    S = S_SMP
    return pl.pallas_call(
        _mixer_sample_kernel,
        grid=(ns // S,),
        in_specs=[
            pl.BlockSpec((nj, S, D_MODEL), lambda s: (0, s, 0)),
            pl.BlockSpec((CONV_HIST, S, C_CONV), lambda s: (0, s, 0)),
            pl.BlockSpec((POOL_HIST, S, C_POOL), lambda s: (0, s, 0)),
        ] + _mixer_weight_specs(),
        out_specs=[
            pl.BlockSpec((nj, S, D_MODEL), lambda s: (0, s, 0)),
            pl.BlockSpec((CONV_HIST, S, C_CONV), lambda s: (0, s, 0)),
            pl.BlockSpec((POOL_HIST, S, C_POOL), lambda s: (0, s, 0)),
        ],
        out_shape=[
            jax.ShapeDtypeStruct((nj, ns, D_MODEL), F32),
            jax.ShapeDtypeStruct((CONV_HIST, ns, C_CONV), F32),
            jax.ShapeDtypeStruct((POOL_HIST, ns, C_POOL), F32),
        ],
        scratch_shapes=[pltpu.VMEM((nj * S, C_CONV + C_POOL), BF16)],
        compiler_params=pltpu.CompilerParams(
            dimension_semantics=("arbitrary",), vmem_limit_bytes=VMEM_LIMIT),
        name="mixer_sample",
    )(xs_t, hist, phist, *mix_w)


def _mlp_kernel(x_ref, gffn_ref, wup_ref, wdown_ref, gfin_ref, o_ref, h_ref):
    j = pl.program_id(1)
    last = pl.num_programs(1) - 1

    def ffn_chunk(h, acc):
        a = jnp.maximum(jnp.dot(h, wup_ref[...].astype(BF16), preferred_element_type=F32), 0.0)
        return acc + jnp.dot((a * a).astype(BF16), wdown_ref[...].astype(BF16), preferred_element_type=F32)

    @pl.when(j == 0)
    def _():
        x = x_ref[...]
        h = _rmsnorm(x, gffn_ref[...]).astype(BF16)
        h_ref[...] = h
        o_ref[...] = ffn_chunk(h, x)

    @pl.when((j > 0) & (j < last))
    def _():
        o_ref[...] = ffn_chunk(h_ref[...], o_ref[...])

    @pl.when(j == last)
    def _():
        o_ref[...] = _rmsnorm(ffn_chunk(h_ref[...], o_ref[...]), gfin_ref[...])


def _mlp(x1, gffn, wup, wdown, gfin):
    n, _ = x1.shape
    tm = min(TM_MLP, n)
    return pl.pallas_call(
        _mlp_kernel,
        grid=(n // tm, D_FF // TF_MLP),
        in_specs=[
            pl.BlockSpec((tm, D_MODEL), lambda i, j: (i, 0), pipeline_mode=pl.Buffered(1)),
            _const_spec((1, D_MODEL)),
            pl.BlockSpec((D_MODEL, TF_MLP), lambda i, j: (0, j)),
            pl.BlockSpec((TF_MLP, D_MODEL), lambda i, j: (j, 0)),
            _const_spec((1, D_MODEL)),
        ],
        out_specs=pl.BlockSpec((tm, D_MODEL), lambda i, j: (i, 0)),
        out_shape=jax.ShapeDtypeStruct((n, D_MODEL), F32),
        scratch_shapes=[pltpu.VMEM((tm, D_MODEL), BF16)],
        compiler_params=pltpu.CompilerParams(
            dimension_semantics=("arbitrary", "arbitrary"), vmem_limit_bytes=VMEM_LIMIT),
        name="mlp",
    )(x1, gffn, wup, wdown, gfin)


def kernel(x_prompt, x_sample, state_conv, state_pool, meta_tokens, norm_mix_g, w_in, w_dw, b_dw, conv_ln_g, conv_ln_b, w_pool, pool_scale, w_out, norm_ffn_g, w_up, w_down, final_norm_g):
    assert norm_mix_g.shape[0] == 1, "single-layer step"
    nb, seq, _ = x_prompt.shape
    ns, nj, _ = x_sample.shape

    gmix = norm_mix_g[0][None]
    gffn = norm_ffn_g[0][None]
    gfin = final_norm_g[None]
    mix_w = (gmix, w_in[0].astype(BF16), w_dw[0], b_dw[0][None],
             conv_ln_g[0][None], conv_ln_b[0][None], w_pool[0].astype(BF16), pool_scale[0][None], w_out[0].astype(BF16))

    x1p, utail, xbtail = _mixer_prompt(x_prompt, meta_tokens, mix_w)
    y_prompt = _mlp(x1p.reshape(nb * seq, D_MODEL), gffn, w_up[0], w_down[0], gfin).reshape(nb, seq, D_MODEL)
    new_conv_prompt = utail[:, CONV_HALO - CONV_HIST:][None]
    new_pool_prompt = xbtail[:, POOL_HALO - POOL_HIST:][None]

    xs_t = jnp.transpose(x_sample, (1, 0, 2))
    hist_t = jnp.transpose(state_conv[0], (1, 0, 2))
    phist_t = jnp.transpose(state_pool[0], (1, 0, 2))
    x1s_t, newc_t, newp_t = _mixer_sample(xs_t, hist_t, phist_t, mix_w)
    ys_t = _mlp(x1s_t.reshape(nj * ns, D_MODEL), gffn, w_up[0], w_down[0], gfin).reshape(nj, ns, D_MODEL)
    y_sample = jnp.transpose(ys_t, (1, 0, 2))
    new_conv_sample = jnp.transpose(newc_t, (1, 0, 2))[None]
    new_pool_sample = jnp.transpose(newp_t, (1, 0, 2))[None]
    return (y_prompt, y_sample, new_conv_prompt, new_pool_prompt, new_conv_sample, new_pool_sample)
```

```python
import jax
import jax.numpy as jnp
from jax import lax
from jax.experimental import pallas as pl
from jax.experimental.pallas import tpu as pltpu

D_MODEL = 2048
C_CONV = 1024
C_POOL = 1024
CONV_WIDTH = 31
CONV_HIST = CONV_WIDTH - 1
POOL_WINDOWS = (2, 4, 8, 16)
N_GRP = len(POOL_WINDOWS)
GRP = C_POOL // N_GRP
POOL_HIST = max(POOL_WINDOWS) - 1
D_FF = 4 * D_MODEL
N_META = 16
EPS = 1e-6

LANES = 128
SUBLANES = 8
N_CTILE = C_CONV // LANES
TILES_PER_GRP = GRP // LANES
CONV_HALO = 32
POOL_HALO = 16

N_WIN = N_GRP + N_GRP // 2
T_MIX = 256
R_CONV = 64
KC_MIX = 256
S_SMP = 32
TM_MLP = 1024
TF_MLP = 512
VMEM_LIMIT = 56 * 1024 * 1024

F32 = jnp.float32
BF16 = jnp.bfloat16


def _rmsnorm(x, g):
    return x * lax.rsqrt(jnp.mean(x * x, axis=-1, keepdims=True) + EPS) * g


def _glu(h, wag_q):
    proj = jnp.dot(h, wag_q, preferred_element_type=F32)
    return proj[:, :GRP] * jax.nn.sigmoid(proj[:, GRP:])


def _head_ln_silu(c, g, b):
    mu = jnp.mean(c, axis=-1, keepdims=True)
    d = c - mu
    var = jnp.mean(d * d, axis=-1, keepdims=True)
    y = d * lax.rsqrt(var + EPS) * g + b
    return y * jax.nn.sigmoid(y)


def _lane_tile(j):
    return slice(j * LANES, (j + 1) * LANES)


def _regroup_win_kernel(w_ref, o_ref):
    for q in range(N_GRP):
        o_ref[q, :, 0:GRP] = w_ref[:, q * GRP:(q + 1) * GRP].astype(BF16)
        o_ref[q, :, GRP:2 * GRP] = w_ref[:, C_CONV + q * GRP:C_CONV + (q + 1) * GRP].astype(BF16)
    for hh in range(N_GRP // 2):
        o_ref[N_GRP + hh] = w_ref[:, 2 * C_CONV + hh * 2 * GRP:2 * C_CONV + (hh + 1) * 2 * GRP].astype(BF16)


def _regroup_win(w_in):
    rows = 256
    return pl.pallas_call(
        _regroup_win_kernel,
        grid=(D_MODEL // rows,),
        in_specs=[pl.BlockSpec((rows, 2 * C_CONV + C_POOL), lambda i: (i, 0))],
        out_specs=pl.BlockSpec((N_WIN, rows, 2 * GRP), lambda i: (0, i, 0)),
        out_shape=jax.ShapeDtypeStruct((N_WIN, D_MODEL, 2 * GRP), BF16),
        compiler_params=pltpu.CompilerParams(dimension_semantics=("arbitrary",)),
        name="regroup_win",
    )(w_in)


def _mixer_prompt_kernel(x_ref, meta_ref, gmix_ref, wag_ref, wdw_ref, bdw_ref, lng_ref, lnb_ref,
                         wpool_ref, pscale_ref, wout_ref,
                         x1_ref, utail_ref, xbtail_ref,
                         hbuf, umeta, xbmeta, wb, *grp_scratch):
    b = pl.program_id(0)
    t = pl.program_id(1)
    T = T_MIX
    ubuf, xbuf, dbuf, cbuf, pbuf = (grp_scratch[i * N_GRP:(i + 1) * N_GRP] for i in range(5))

    def put(dst, v, row0):
        for jj in range(TILES_PER_GRP):
            dst[jj, row0:row0 + v.shape[0], :] = v[:, _lane_tile(jj)]

    @pl.when((b == 0) & (t == 0))
    def _():
        hm = _rmsnorm(meta_ref[...], gmix_ref[...]).astype(BF16)
        for q in range(N_GRP):
            put(umeta.at[q], _glu(hm, wag_ref[q]), 0)
        for hh in range(N_GRP // 2):
            xm = jnp.dot(hm, wag_ref[N_GRP + hh], preferred_element_type=F32)
            put(xbmeta.at[2 * hh], xm[:, :GRP], 0)
            put(xbmeta.at[2 * hh + 1], xm[:, GRP:], 0)
        for k in range(CONV_WIDTH):
            wb[k] = jnp.broadcast_to(wdw_ref[k:k + 1, :], (SUBLANES, C_CONV))

    @pl.when(t == 0)
    def _():
        for q in range(N_GRP):
            ubuf[q][:, 0:CONV_HALO - N_META, :] = jnp.zeros((TILES_PER_GRP, CONV_HALO - N_META, LANES), F32)
            ubuf[q][:, CONV_HALO - N_META:CONV_HALO, :] = umeta[q]
            xbuf[q][:, 0:POOL_HALO, :] = xbmeta[q]

    def proj_xb():
        for hh in range(N_GRP // 2):
            acc = None
            for k0 in range(0, D_MODEL, KC_MIX):
                d = jnp.dot(hbuf[:, k0:k0 + KC_MIX], wag_ref[N_GRP + hh, k0:k0 + KC_MIX, :], preferred_element_type=F32)
                acc = d if acc is None else acc + d
                yield
            put(xbuf[2 * hh], acc[:, :GRP], POOL_HALO)
            put(xbuf[2 * hh + 1], acc[:, GRP:], POOL_HALO)
            yield

    def proj_u(q):
        acc = None
        for k0 in range(0, D_MODEL, KC_MIX):
            d = jnp.dot(hbuf[:, k0:k0 + KC_MIX], wag_ref[q, k0:k0 + KC_MIX, :], preferred_element_type=F32)
            acc = d if acc is None else acc + d
            yield
        put(ubuf[q], acc[:, :GRP] * jax.nn.sigmoid(acc[:, GRP:]), CONV_HALO)
        yield

    def conv(q):
        for jj in range(TILES_PER_GRP):
            lanes = _lane_tile(q * TILES_PER_GRP + jj)
            for r0 in range(0, T, R_CONV):
                acc = jnp.broadcast_to(bdw_ref[:, lanes], (R_CONV, LANES))
                for k in range(CONV_WIDTH):
                    row = r0 + (CONV_HALO - CONV_HIST) + k
                    acc = acc + ubuf[q][jj, row:row + R_CONV, :] * jnp.tile(wb[k, :, lanes], (R_CONV // SUBLANES, 1))
                c = _head_ln_silu(acc, lng_ref[:, lanes], lnb_ref[:, lanes])
                cbuf[q][r0:r0 + R_CONV, _lane_tile(jj)] = c.astype(BF16)
                yield

    def pool(q):
        w = POOL_WINDOWS[q]
        for jj in range(TILES_PER_GRP):
            for r0 in range(0, T, R_CONV):
                cur = xbuf[q][jj, POOL_HALO + r0:POOL_HALO + r0 + R_CONV, :]
                s = cur
                for d in range(1, w):
                    s = s + xbuf[q][jj, POOL_HALO + r0 - d:POOL_HALO + r0 - d + R_CONV, :]
                dbuf[q][r0:r0 + R_CONV, _lane_tile(jj)] = (s * (1.0 / w) - cur).astype(BF16)
            yield

    def pool_proj(q):
        p = jnp.dot(dbuf[q][...], wpool_ref[q], preferred_element_type=F32) * pscale_ref[:, q * GRP:(q + 1) * GRP]
        pbuf[q][...] = p.astype(BF16)
        yield

    def wout(bufs, k0):
        acc = x1_ref[0]
        for i, buf in enumerate(bufs):
            acc = acc + jnp.dot(buf[...], wout_ref[k0 + i * GRP:k0 + (i + 1) * GRP, :], preferred_element_type=F32)
            yield
        x1_ref[0] = acc
        yield

    def seq(*gens):
        for g in gens:
            yield from g

    def interleave(a, b):
        live = [a, b]
        while live:
            for g in list(live):
                if next(g, StopIteration) is StopIteration:
                    live.remove(g)

    x = x_ref[0]
    x1_ref[0] = x
    hbuf[...] = _rmsnorm(x, gmix_ref[...]).astype(BF16)
    interleave(proj_u(0), iter(()))
    interleave(proj_u(1), conv(0))
    interleave(proj_u(2), conv(1))
    interleave(proj_u(3), conv(2))
    interleave(proj_xb(), conv(3))
    interleave(seq(*[pool(q) for q in range(N_GRP)], *[pool_proj(q) for q in range(N_GRP)], wout(pbuf, C_CONV)),
               wout(cbuf[0:4], 0))

    @pl.when(t == pl.num_programs(1) - 1)
    def _():
        for q in range(N_GRP):
            for jj in range(TILES_PER_GRP):
                lanes = _lane_tile(q * TILES_PER_GRP + jj)
                utail_ref[0, :, lanes] = ubuf[q][jj, T:T + CONV_HALO, :]
                xbtail_ref[0, :, lanes] = xbuf[q][jj, T:T + POOL_HALO, :]

    for q in range(N_GRP):
        ubuf[q][:, 0:CONV_HALO, :] = ubuf[q][:, T:T + CONV_HALO, :]
        xbuf[q][:, 0:POOL_HALO, :] = xbuf[q][:, T:T + POOL_HALO, :]


def _const_spec(shape):
    return pl.BlockSpec(shape, lambda *_: (0,) * len(shape), pipeline_mode=pl.Buffered(1))


def _mixer_weight_specs():
    return [
        _const_spec((1, D_MODEL)),
        _const_spec((N_WIN, D_MODEL, 2 * GRP)),
        _const_spec((CONV_WIDTH, C_CONV)),
        _const_spec((1, C_CONV)),
        _const_spec((1, C_CONV)),
        _const_spec((1, C_CONV)),
        _const_spec((N_GRP, GRP, GRP)),
        _const_spec((1, C_POOL)),
        _const_spec((C_CONV + C_POOL, D_MODEL)),
    ]


def _mixer_prompt(x_prompt, meta, mix_w):
    nb, seq, _ = x_prompt.shape
    T = T_MIX
    return pl.pallas_call(
        _mixer_prompt_kernel,
        grid=(nb, seq // T),
        in_specs=[
            pl.BlockSpec((1, T, D_MODEL), lambda b, t: (b, t, 0)),
            _const_spec((N_META, D_MODEL)),
        ] + _mixer_weight_specs(),
        out_specs=[
            pl.BlockSpec((1, T, D_MODEL), lambda b, t: (b, t, 0)),
            pl.BlockSpec((1, CONV_HALO, C_CONV), lambda b, t: (b, 0, 0)),
            pl.BlockSpec((1, POOL_HALO, C_POOL), lambda b, t: (b, 0, 0)),
        ],
        out_shape=[
            jax.ShapeDtypeStruct((nb, seq, D_MODEL), F32),
            jax.ShapeDtypeStruct((nb, CONV_HALO, C_CONV), F32),
            jax.ShapeDtypeStruct((nb, POOL_HALO, C_POOL), F32),
        ],
        scratch_shapes=[
            pltpu.VMEM((T, D_MODEL), BF16),
            pltpu.VMEM((N_GRP, TILES_PER_GRP, N_META, LANES), F32),
            pltpu.VMEM((N_GRP, TILES_PER_GRP, N_META, LANES), F32),
            pltpu.VMEM((CONV_WIDTH, SUBLANES, C_CONV), F32),
        ]
        + [pltpu.VMEM((TILES_PER_GRP, CONV_HALO + T, LANES), F32)] * N_GRP
        + [pltpu.VMEM((TILES_PER_GRP, POOL_HALO + T, LANES), F32)] * N_GRP
        + [pltpu.VMEM((T, GRP), BF16)] * N_GRP
        + [pltpu.VMEM((T, GRP), BF16)] * N_GRP
        + [pltpu.VMEM((T, GRP), BF16)] * N_GRP,
        compiler_params=pltpu.CompilerParams(
            dimension_semantics=("arbitrary", "arbitrary"), vmem_limit_bytes=VMEM_LIMIT),
        name="mixer_prompt",
    )(x_prompt, meta, *mix_w)


def _mixer_sample_kernel(xs_ref, hist_ref, phist_ref, gmix_ref, wag_ref, wdw_ref, bdw_ref, lng_ref, lnb_ref,
                         wpool_ref, pscale_ref, wout_ref,
                         x1_ref, newc_ref, newp_ref, cp):
    S = S_SMP
    nj = xs_ref.shape[0]
    n_keep_c = CONV_HIST - nj
    n_keep_p = POOL_HIST - nj
    x = jnp.concatenate([xs_ref[j] for j in range(nj)], axis=0)
    h = _rmsnorm(x, gmix_ref[...]).astype(BF16)
    newc_ref[0:n_keep_c] = hist_ref[nj:CONV_HIST]
    newp_ref[0:n_keep_p] = phist_ref[nj:POOL_HIST]
    for q in range(N_GRP):
        u = _glu(h, wag_ref[q])
        for j in range(nj):
            newc_ref[n_keep_c + j, :, q * GRP:(q + 1) * GRP] = u[j * S:(j + 1) * S]
    for hh in range(N_GRP // 2):
        xb = jnp.dot(h, wag_ref[N_GRP + hh], preferred_element_type=F32)
        for j in range(nj):
            newp_ref[n_keep_p + j, :, hh * 2 * GRP:(hh + 1) * 2 * GRP] = xb[j * S:(j + 1) * S]

    def conv_tile(lt, carry):
        lane0 = pl.multiple_of(lt * LANES, LANES)
        lanes = pl.ds(lane0, LANES)
        bias = jnp.broadcast_to(bdw_ref[:, lanes], (S, LANES))
        acc = [bias for _ in range(nj)]
        for kk in range(CONV_HIST + nj):
            row = hist_ref[kk, :, lanes] if kk < CONV_HIST else newc_ref[kk - nj, :, lanes]
            for j in range(nj):
                k = kk - j
                if 0 <= k < CONV_WIDTH:
                    acc[j] = acc[j] + row * wdw_ref[k:k + 1, lanes]
        for j in range(nj):
            c = _head_ln_silu(acc[j], lng_ref[:, lanes], lnb_ref[:, lanes])
            cp[pl.ds(j * S, S), lanes] = c.astype(BF16)
        return carry

    lax.fori_loop(0, N_CTILE, conv_tile, 0)

    for gi, w in enumerate(POOL_WINDOWS):
        sl = slice(gi * GRP, (gi + 1) * GRP)
        rows = ([phist_ref[i, :, sl] for i in range(POOL_HIST - w + 1, POOL_HIST)]
                + [newp_ref[n_keep_p + j, :, sl] for j in range(nj)])
        ds = []
        for j in range(nj):
            win_rows = rows[j:j + w]
            s = win_rows[0]
            for r in win_rows[1:]:
                s = s + r
            ds.append(s * (1.0 / w) - win_rows[-1])
        d = jnp.concatenate(ds, axis=0).astype(BF16)
        p = jnp.dot(d, wpool_ref[gi], preferred_element_type=F32) * pscale_ref[:, sl]
        cp[:, C_CONV + gi * GRP:C_CONV + (gi + 1) * GRP] = p.astype(BF16)

    y = jnp.dot(cp[...], wout_ref[...], preferred_element_type=F32)
    x1 = x + y
    for j in range(nj):
        x1_ref[j] = x1[j * S:(j + 1) * S]


def _mixer_sample(xs_t, hist, phist, mix_w):
    nj, ns, _ = xs_t.shape
    S = S_SMP
    return pl.pallas_call(
        _mixer_sample_kernel,
        grid=(ns // S,),
        in_specs=[
            pl.BlockSpec((nj, S, D_MODEL), lambda s: (0, s, 0)),
            pl.BlockSpec((CONV_HIST, S, C_CONV), lambda s: (0, s, 0)),
            pl.BlockSpec((POOL_HIST, S, C_POOL), lambda s: (0, s, 0)),
        ] + _mixer_weight_specs(),
        out_specs=[
            pl.BlockSpec((nj, S, D_MODEL), lambda s: (0, s, 0)),
            pl.BlockSpec((CONV_HIST, S, C_CONV), lambda s: (0, s, 0)),
            pl.BlockSpec((POOL_HIST, S, C_POOL), lambda s: (0, s, 0)),
        ],
        out_shape=[
            jax.ShapeDtypeStruct((nj, ns, D_MODEL), F32),
            jax.ShapeDtypeStruct((CONV_HIST, ns, C_CONV), F32),
            jax.ShapeDtypeStruct((POOL_HIST, ns, C_POOL), F32),
        ],
        scratch_shapes=[pltpu.VMEM((nj * S, C_CONV + C_POOL), BF16)],
        compiler_params=pltpu.CompilerParams(
            dimension_semantics=("arbitrary",), vmem_limit_bytes=VMEM_LIMIT),
        name="mixer_sample",
    )(xs_t, hist, phist, *mix_w)


def _mlp_kernel(x_ref, gffn_ref, wup_ref, wdown_ref, gfin_ref, o_ref, h_ref):
    j = pl.program_id(1)
    last = pl.num_programs(1) - 1

    def ffn_chunk(h, acc):
        a = jnp.maximum(jnp.dot(h, wup_ref[...].astype(BF16), preferred_element_type=F32), 0.0)
        return acc + jnp.dot((a * a).astype(BF16), wdown_ref[...].astype(BF16), preferred_element_type=F32)

    @pl.when(j == 0)
    def _():
        x = x_ref[...]
        h = _rmsnorm(x, gffn_ref[...]).astype(BF16)
        h_ref[...] = h
        o_ref[...] = ffn_chunk(h, x)

    @pl.when((j > 0) & (j < last))
    def _():
        o_ref[...] = ffn_chunk(h_ref[...], o_ref[...])

    @pl.when(j == last)
    def _():
        o_ref[...] = _rmsnorm(ffn_chunk(h_ref[...], o_ref[...]), gfin_ref[...])


def _mlp(x1, gffn, wup, wdown, gfin):
    n, _ = x1.shape
    tm = min(TM_MLP, n)
    tf = TF_MLP * (TM_MLP // tm)
    return pl.pallas_call(
        _mlp_kernel,
        grid=(n // tm, D_FF // tf),
        in_specs=[
            pl.BlockSpec((tm, D_MODEL), lambda i, j: (i, 0), pipeline_mode=pl.Buffered(1)),
            _const_spec((1, D_MODEL)),
            pl.BlockSpec((D_MODEL, tf), lambda i, j: (0, j)),
            pl.BlockSpec((tf, D_MODEL), lambda i, j: (j, 0)),
            _const_spec((1, D_MODEL)),
        ],
        out_specs=pl.BlockSpec((tm, D_MODEL), lambda i, j: (i, 0)),
        out_shape=jax.ShapeDtypeStruct((n, D_MODEL), F32),
        scratch_shapes=[pltpu.VMEM((tm, D_MODEL), BF16)],
        compiler_params=pltpu.CompilerParams(
            dimension_semantics=("arbitrary", "arbitrary"), vmem_limit_bytes=VMEM_LIMIT),
        name="mlp",
    )(x1, gffn, wup, wdown, gfin)


def kernel(x_prompt, x_sample, state_conv, state_pool, meta_tokens, norm_mix_g, w_in, w_dw, b_dw, conv_ln_g, conv_ln_b, w_pool, pool_scale, w_out, norm_ffn_g, w_up, w_down, final_norm_g):
    assert norm_mix_g.shape[0] == 1, "single-layer step"
    nb, seq, _ = x_prompt.shape
    ns, nj, _ = x_sample.shape

    gmix = norm_mix_g[0][None]
    gffn = norm_ffn_g[0][None]
    gfin = final_norm_g[None]
    mix_w = (gmix, _regroup_win(w_in[0]), w_dw[0], b_dw[0][None],
             conv_ln_g[0][None], conv_ln_b[0][None], w_pool[0].astype(BF16), pool_scale[0][None], w_out[0].astype(BF16))

    x1p, utail, xbtail = _mixer_prompt(x_prompt, meta_tokens, mix_w)
    y_prompt = _mlp(x1p.reshape(nb * seq, D_MODEL), gffn, w_up[0], w_down[0], gfin).reshape(nb, seq, D_MODEL)
    new_conv_prompt = utail[:, CONV_HALO - CONV_HIST:][None]
    new_pool_prompt = xbtail[:, POOL_HALO - POOL_HIST:][None]

    xs_t = jnp.transpose(x_sample, (1, 0, 2))
    hist_t = jnp.transpose(state_conv[0], (1, 0, 2))
    phist_t = jnp.transpose(state_pool[0], (1, 0, 2))
    x1s_t, newc_t, newp_t = _mixer_sample(xs_t, hist_t, phist_t, mix_w)
    ys_t = _mlp(x1s_t.reshape(nj * ns, D_MODEL), gffn, w_up[0], w_down[0], gfin).reshape(nj, ns, D_MODEL)
    y_sample = jnp.transpose(ys_t, (1, 0, 2))
    new_conv_sample = jnp.transpose(newc_t, (1, 0, 2))[None]
    new_pool_sample = jnp.transpose(newp_t, (1, 0, 2))[None]
    return (y_prompt, y_sample, new_conv_prompt, new_pool_prompt, new_conv_sample, new_pool_sample)
```

```python
import jax
import jax.numpy as jnp
from jax import lax
from jax.experimental import pallas as pl
from jax.experimental.pallas import tpu as pltpu

D_MODEL = 2048
C_CONV = 1024
C_POOL = 1024
CONV_WIDTH = 31
CONV_HIST = CONV_WIDTH - 1
POOL_WINDOWS = (2, 4, 8, 16)
N_GRP = len(POOL_WINDOWS)
GRP = C_POOL // N_GRP
POOL_HIST = max(POOL_WINDOWS) - 1
D_FF = 4 * D_MODEL
N_META = 16
EPS = 1e-6

LANES = 128
SUBLANES = 8
N_CTILE = C_CONV // LANES
TILES_PER_GRP = GRP // LANES
CONV_HALO = 32
POOL_HALO = 16

N_WIN = N_GRP + N_GRP // 2
T_MIX = 256
R_CONV = 64
S_SMP = 32
TM_MLP = 1024
TF_MLP = 512
VMEM_LIMIT = 56 * 1024 * 1024

F32 = jnp.float32
BF16 = jnp.bfloat16


def _rmsnorm(x, g):
    return x * lax.rsqrt(jnp.mean(x * x, axis=-1, keepdims=True) + EPS) * g


def _glu(h, wag_q):
    proj = jnp.dot(h, wag_q, preferred_element_type=F32)
    return proj[:, :GRP] * jax.nn.sigmoid(proj[:, GRP:])


def _head_ln_silu(c, g, b):
    mu = jnp.mean(c, axis=-1, keepdims=True)
    d = c - mu
    var = jnp.mean(d * d, axis=-1, keepdims=True)
    y = d * lax.rsqrt(var + EPS) * g + b
    return y * jax.nn.sigmoid(y)


def _lane_tile(j):
    return slice(j * LANES, (j + 1) * LANES)


def _regroup_win_kernel(w_ref, o_ref):
    for q in range(N_GRP):
        o_ref[q, :, 0:GRP] = w_ref[:, q * GRP:(q + 1) * GRP].astype(BF16)
        o_ref[q, :, GRP:2 * GRP] = w_ref[:, C_CONV + q * GRP:C_CONV + (q + 1) * GRP].astype(BF16)
    for hh in range(N_GRP // 2):
        o_ref[N_GRP + hh] = w_ref[:, 2 * C_CONV + hh * 2 * GRP:2 * C_CONV + (hh + 1) * 2 * GRP].astype(BF16)


def _regroup_win(w_in):
    rows = 256
    return pl.pallas_call(
        _regroup_win_kernel,
        grid=(D_MODEL // rows,),
        in_specs=[pl.BlockSpec((rows, 2 * C_CONV + C_POOL), lambda i: (i, 0))],
        out_specs=pl.BlockSpec((N_WIN, rows, 2 * GRP), lambda i: (0, i, 0)),
        out_shape=jax.ShapeDtypeStruct((N_WIN, D_MODEL, 2 * GRP), BF16),
        compiler_params=pltpu.CompilerParams(dimension_semantics=("arbitrary",)),
        name="regroup_win",
    )(w_in)


def _mixer_prompt_kernel(x_ref, meta_ref, gmix_ref, wag_ref, wdw_ref, bdw_ref, lng_ref, lnb_ref,
                         wpool_ref, pscale_ref, wout_ref,
                         x1_ref, utail_ref, xbtail_ref,
                         ubuf, xbuf, umeta, xbmeta, wb, dbuf, cp):
    b = pl.program_id(0)
    t = pl.program_id(1)
    T = T_MIX

    def project(h, u_dst, u_row0, xb_dst, xb_row0):
        def put(dst, j0, row0, v):
            for i in range(v.shape[1] // LANES):
                dst[j0 + i, row0:row0 + v.shape[0], :] = v[:, _lane_tile(i)]
        for q in range(N_GRP):
            put(u_dst, q * TILES_PER_GRP, u_row0, _glu(h, wag_ref[q]))
        for hh in range(N_GRP // 2):
            xb = jnp.dot(h, wag_ref[N_GRP + hh], preferred_element_type=F32)
            put(xb_dst, hh * 2 * TILES_PER_GRP, xb_row0, xb)

    @pl.when((b == 0) & (t == 0))
    def _():
        project(_rmsnorm(meta_ref[...], gmix_ref[...]).astype(BF16), umeta, 0, xbmeta, 0)
        for k in range(CONV_WIDTH):
            wb[k] = jnp.broadcast_to(wdw_ref[k:k + 1, :], (SUBLANES, C_CONV))

    @pl.when(t == 0)
    def _():
        ubuf[:, 0:CONV_HALO - N_META, :] = jnp.zeros((N_CTILE, CONV_HALO - N_META, LANES), F32)
        ubuf[:, CONV_HALO - N_META:CONV_HALO, :] = umeta[...]
        xbuf[:, 0:POOL_HALO, :] = xbmeta[...]

    x = x_ref[0]
    project(_rmsnorm(x, gmix_ref[...]).astype(BF16), ubuf, CONV_HALO, xbuf, POOL_HALO)

    for j in range(N_CTILE):
        lanes = _lane_tile(j)
        w = POOL_WINDOWS[j // TILES_PER_GRP]
        for r0 in range(0, T, R_CONV):
            acc = jnp.broadcast_to(bdw_ref[:, lanes], (R_CONV, LANES))
            for k in range(CONV_WIDTH):
                row = r0 + (CONV_HALO - CONV_HIST) + k
                acc = acc + ubuf[j, row:row + R_CONV, :] * jnp.tile(wb[k, :, lanes], (R_CONV // SUBLANES, 1))
            cp[r0:r0 + R_CONV, lanes] = _head_ln_silu(acc, lng_ref[:, lanes], lnb_ref[:, lanes]).astype(BF16)
            cur = xbuf[j, POOL_HALO + r0:POOL_HALO + r0 + R_CONV, :]
            s = cur
            for d in range(1, w):
                s = s + xbuf[j, POOL_HALO + r0 - d:POOL_HALO + r0 - d + R_CONV, :]
            dbuf[r0:r0 + R_CONV, lanes] = (s * (1.0 / w) - cur).astype(BF16)

    for q in range(N_GRP):
        gsl = slice(q * GRP, (q + 1) * GRP)
        p = jnp.dot(dbuf[:, gsl], wpool_ref[q], preferred_element_type=F32) * pscale_ref[:, gsl]
        cp[:, C_CONV + q * GRP:C_CONV + (q + 1) * GRP] = p.astype(BF16)

    x1_ref[0] = x + jnp.dot(cp[...], wout_ref[...], preferred_element_type=F32)

    @pl.when(t == pl.num_programs(1) - 1)
    def _():
        for j in range(N_CTILE):
            utail_ref[0, :, _lane_tile(j)] = ubuf[j, T:T + CONV_HALO, :]
            xbtail_ref[0, :, _lane_tile(j)] = xbuf[j, T:T + POOL_HALO, :]

    ubuf[:, 0:CONV_HALO, :] = ubuf[:, T:T + CONV_HALO, :]
    xbuf[:, 0:POOL_HALO, :] = xbuf[:, T:T + POOL_HALO, :]


def _const_spec(shape):
    return pl.BlockSpec(shape, lambda *_: (0,) * len(shape), pipeline_mode=pl.Buffered(1))


def _mixer_weight_specs():
    return [
        _const_spec((1, D_MODEL)),
        _const_spec((N_WIN, D_MODEL, 2 * GRP)),
        _const_spec((CONV_WIDTH, C_CONV)),
        _const_spec((1, C_CONV)),
        _const_spec((1, C_CONV)),
        _const_spec((1, C_CONV)),
        _const_spec((N_GRP, GRP, GRP)),
        _const_spec((1, C_POOL)),
        _const_spec((C_CONV + C_POOL, D_MODEL)),
    ]


def _mixer_prompt(x_prompt, meta, mix_w):
    nb, seq, _ = x_prompt.shape
    T = T_MIX
    return pl.pallas_call(
        _mixer_prompt_kernel,
        grid=(nb, seq // T),
        in_specs=[
            pl.BlockSpec((1, T, D_MODEL), lambda b, t: (b, t, 0)),
            _const_spec((N_META, D_MODEL)),
        ] + _mixer_weight_specs(),
        out_specs=[
            pl.BlockSpec((1, T, D_MODEL), lambda b, t: (b, t, 0)),
            pl.BlockSpec((1, CONV_HALO, C_CONV), lambda b, t: (b, 0, 0)),
            pl.BlockSpec((1, POOL_HALO, C_POOL), lambda b, t: (b, 0, 0)),
        ],
        out_shape=[
            jax.ShapeDtypeStruct((nb, seq, D_MODEL), F32),
            jax.ShapeDtypeStruct((nb, CONV_HALO, C_CONV), F32),
            jax.ShapeDtypeStruct((nb, POOL_HALO, C_POOL), F32),
        ],
        scratch_shapes=[
            pltpu.VMEM((N_CTILE, CONV_HALO + T, LANES), F32),
            pltpu.VMEM((N_CTILE, POOL_HALO + T, LANES), F32),
            pltpu.VMEM((N_CTILE, N_META, LANES), F32),
            pltpu.VMEM((N_CTILE, N_META, LANES), F32),
            pltpu.VMEM((CONV_WIDTH, SUBLANES, C_CONV), F32),
            pltpu.VMEM((T, C_POOL), BF16),
            pltpu.VMEM((T, C_CONV + C_POOL), BF16),
        ],
        compiler_params=pltpu.CompilerParams(
            dimension_semantics=("arbitrary", "arbitrary"), vmem_limit_bytes=VMEM_LIMIT),
        name="mixer_prompt",
    )(x_prompt, meta, *mix_w)


def _mixer_sample_kernel(xs_ref, hist_ref, phist_ref, gmix_ref, wag_ref, wdw_ref, bdw_ref, lng_ref, lnb_ref,
                         wpool_ref, pscale_ref, wout_ref,
                         x1_ref, newc_ref, newp_ref, cp):
    S = S_SMP
    nj = xs_ref.shape[0]
    n_keep_c = CONV_HIST - nj
    n_keep_p = POOL_HIST - nj
    x = jnp.concatenate([xs_ref[j] for j in range(nj)], axis=0)
    h = _rmsnorm(x, gmix_ref[...]).astype(BF16)
    newc_ref[0:n_keep_c] = hist_ref[nj:CONV_HIST]
    newp_ref[0:n_keep_p] = phist_ref[nj:POOL_HIST]
    for q in range(N_GRP):
        u = _glu(h, wag_ref[q])
        for j in range(nj):
            newc_ref[n_keep_c + j, :, q * GRP:(q + 1) * GRP] = u[j * S:(j + 1) * S]
    for hh in range(N_GRP // 2):
        xb = jnp.dot(h, wag_ref[N_GRP + hh], preferred_element_type=F32)
        for j in range(nj):
            newp_ref[n_keep_p + j, :, hh * 2 * GRP:(hh + 1) * 2 * GRP] = xb[j * S:(j + 1) * S]

    def conv_tile(lt, carry):
        lane0 = pl.multiple_of(lt * LANES, LANES)
        lanes = pl.ds(lane0, LANES)
        bias = jnp.broadcast_to(bdw_ref[:, lanes], (S, LANES))
        acc = [bias for _ in range(nj)]
        for kk in range(CONV_HIST + nj):
            row = hist_ref[kk, :, lanes] if kk < CONV_HIST else newc_ref[kk - nj, :, lanes]
            for j in range(nj):
                k = kk - j
                if 0 <= k < CONV_WIDTH:
                    acc[j] = acc[j] + row * wdw_ref[k:k + 1, lanes]
        for j in range(nj):
            c = _head_ln_silu(acc[j], lng_ref[:, lanes], lnb_ref[:, lanes])
            cp[pl.ds(j * S, S), lanes] = c.astype(BF16)
        return carry

    lax.fori_loop(0, N_CTILE, conv_tile, 0)

    for gi, w in enumerate(POOL_WINDOWS):
        sl = slice(gi * GRP, (gi + 1) * GRP)
        rows = ([phist_ref[i, :, sl] for i in range(POOL_HIST - w + 1, POOL_HIST)]
                + [newp_ref[n_keep_p + j, :, sl] for j in range(nj)])
        ds = []
        for j in range(nj):
            win_rows = rows[j:j + w]
            s = win_rows[0]
            for r in win_rows[1:]:
                s = s + r
            ds.append(s * (1.0 / w) - win_rows[-1])
        d = jnp.concatenate(ds, axis=0).astype(BF16)
        p = jnp.dot(d, wpool_ref[gi], preferred_element_type=F32) * pscale_ref[:, sl]
        cp[:, C_CONV + gi * GRP:C_CONV + (gi + 1) * GRP] = p.astype(BF16)

    y = jnp.dot(cp[...], wout_ref[...], preferred_element_type=F32)
    x1 = x + y
    for j in range(nj):
        x1_ref[j] = x1[j * S:(j + 1) * S]


def _mixer_sample(xs_t, hist, phist, mix_w):
    nj, ns, _ = xs_t.shape
    S = S_SMP
    return pl.pallas_call(
        _mixer_sample_kernel,
        grid=(ns // S,),
        in_specs=[
            pl.BlockSpec((nj, S, D_MODEL), lambda s: (0, s, 0)),
            pl.BlockSpec((CONV_HIST, S, C_CONV), lambda s: (0, s, 0)),
            pl.BlockSpec((POOL_HIST, S, C_POOL), lambda s: (0, s, 0)),
        ] + _mixer_weight_specs(),
        out_specs=[
            pl.BlockSpec((nj, S, D_MODEL), lambda s: (0, s, 0)),
            pl.BlockSpec((CONV_HIST, S, C_CONV), lambda s: (0, s, 0)),
            pl.BlockSpec((POOL_HIST, S, C_POOL), lambda s: (0, s, 0)),
        ],
        out_shape=[
            jax.ShapeDtypeStruct((nj, ns, D_MODEL), F32),
            jax.ShapeDtypeStruct((CONV_HIST, ns, C_CONV), F32),
            jax.ShapeDtypeStruct((POOL_HIST, ns, C_POOL), F32),
        ],
        scratch_shapes=[pltpu.VMEM((nj * S, C_CONV + C_POOL), BF16)],
        compiler_params=pltpu.CompilerParams(
            dimension_semantics=("arbitrary",), vmem_limit_bytes=VMEM_LIMIT),
        name="mixer_sample",
    )(xs_t, hist, phist, *mix_w)


def _mlp_kernel(x_ref, gffn_ref, wup_ref, wdown_ref, gfin_ref, o_ref, h_ref):
    j = pl.program_id(1)
    last = pl.num_programs(1) - 1

    def ffn_chunk(h, acc):
        a = jnp.maximum(jnp.dot(h, wup_ref[...].astype(BF16), preferred_element_type=F32), 0.0)
        return acc + jnp.dot((a * a).astype(BF16), wdown_ref[...].astype(BF16), preferred_element_type=F32)

    @pl.when(j == 0)
    def _():
        x = x_ref[...]
        h = _rmsnorm(x, gffn_ref[...]).astype(BF16)
        h_ref[...] = h
        o_ref[...] = ffn_chunk(h, x)

    @pl.when((j > 0) & (j < last))
    def _():
        o_ref[...] = ffn_chunk(h_ref[...], o_ref[...])

    @pl.when(j == last)
    def _():
        o_ref[...] = _rmsnorm(ffn_chunk(h_ref[...], o_ref[...]), gfin_ref[...])


def _mlp(x1, gffn, wup, wdown, gfin):
    n, _ = x1.shape
    tm = min(TM_MLP, n)
    tf = TF_MLP * (TM_MLP // tm)
    return pl.pallas_call(
        _mlp_kernel,
        grid=(n // tm, D_FF // tf),
        in_specs=[
            pl.BlockSpec((tm, D_MODEL), lambda i, j: (i, 0), pipeline_mode=pl.Buffered(1)),
            _const_spec((1, D_MODEL)),
            pl.BlockSpec((D_MODEL, tf), lambda i, j: (0, j)),
            pl.BlockSpec((tf, D_MODEL), lambda i, j: (j, 0)),
            _const_spec((1, D_MODEL)),
        ],
        out_specs=pl.BlockSpec((tm, D_MODEL), lambda i, j: (i, 0)),
        out_shape=jax.ShapeDtypeStruct((n, D_MODEL), F32),
        scratch_shapes=[pltpu.VMEM((tm, D_MODEL), BF16)],
        compiler_params=pltpu.CompilerParams(
            dimension_semantics=("arbitrary", "arbitrary"), vmem_limit_bytes=VMEM_LIMIT),
        name="mlp",
    )(x1, gffn, wup, wdown, gfin)


def kernel(x_prompt, x_sample, state_conv, state_pool, meta_tokens, norm_mix_g, w_in, w_dw, b_dw, conv_ln_g, conv_ln_b, w_pool, pool_scale, w_out, norm_ffn_g, w_up, w_down, final_norm_g):
    assert norm_mix_g.shape[0] == 1, "single-layer step"
    nb, seq, _ = x_prompt.shape
    ns, nj, _ = x_sample.shape

    gmix = norm_mix_g[0][None]
    gffn = norm_ffn_g[0][None]
    gfin = final_norm_g[None]
    mix_w = (gmix, _regroup_win(w_in[0]), w_dw[0], b_dw[0][None],
             conv_ln_g[0][None], conv_ln_b[0][None], w_pool[0].astype(BF16), pool_scale[0][None], w_out[0].astype(BF16))

    x1p, utail, xbtail = _mixer_prompt(x_prompt, meta_tokens, mix_w)
    y_prompt = _mlp(x1p.reshape(nb * seq, D_MODEL), gffn, w_up[0], w_down[0], gfin).reshape(nb, seq, D_MODEL)
    new_conv_prompt = utail[:, CONV_HALO - CONV_HIST:][None]
    new_pool_prompt = xbtail[:, POOL_HALO - POOL_HIST:][None]

    xs_t = jnp.transpose(x_sample, (1, 0, 2))
    hist_t = jnp.transpose(state_conv[0], (1, 0, 2))
    phist_t = jnp.transpose(state_pool[0], (1, 0, 2))
    x1s_t, newc_t, newp_t = _mixer_sample(xs_t, hist_t, phist_t, mix_w)
    ys_t = _mlp(x1s_t.reshape(nj * ns, D_MODEL), gffn, w_up[0], w_down[0], gfin).reshape(nj, ns, D_MODEL)
    y_sample = jnp.transpose(ys_t, (1, 0, 2))
    new_conv_sample = jnp.transpose(newc_t, (1, 0, 2))[None]
    new_pool_sample = jnp.transpose(newp_t, (1, 0, 2))[None]
    return (y_prompt, y_sample, new_conv_prompt, new_pool_prompt, new_conv_sample, new_pool_sample)
```

```python
import jax
import jax.numpy as jnp
from jax import lax
from jax.experimental import pallas as pl
from jax.experimental.pallas import tpu as pltpu

D_MODEL = 2048
C_CONV = 1024
C_POOL = 1024
CONV_WIDTH = 31
CONV_HIST = CONV_WIDTH - 1
POOL_WINDOWS = (2, 4, 8, 16)
N_GRP = len(POOL_WINDOWS)
GRP = C_POOL // N_GRP
POOL_HIST = max(POOL_WINDOWS) - 1
D_FF = 4 * D_MODEL
N_META = 16
EPS = 1e-6

LANES = 128
SUBLANES = 8
N_CTILE = C_CONV // LANES
TILES_PER_GRP = GRP // LANES
CONV_HALO = 32
POOL_HALO = 16

N_WIN = N_GRP + N_GRP // 2
T_MIX = 256
R_CONV = 64
S_SMP = 32
TM_MLP = 1024
TF_MLP = 512
VMEM_BYTES = 64 * 1024 * 1024
VMEM_LIMIT = VMEM_BYTES - 3 * 1024 * 1024

F32 = jnp.float32
BF16 = jnp.bfloat16


def _rmsnorm(x, g):
    return x * lax.rsqrt(jnp.mean(x * x, axis=-1, keepdims=True) + EPS) * g


def _glu(h, wag_q):
    proj = jnp.dot(h, wag_q, preferred_element_type=F32)
    return proj[:, :GRP] * jax.nn.sigmoid(proj[:, GRP:])


def _head_ln_silu(c, g, b):
    mu = jnp.mean(c, axis=-1, keepdims=True)
    d = c - mu
    var = jnp.mean(d * d, axis=-1, keepdims=True)
    y = d * lax.rsqrt(var + EPS) * g + b
    return y * jax.nn.sigmoid(y)


def _lane_tile(j):
    return slice(j * LANES, (j + 1) * LANES)


def _regroup_win_kernel(w_ref, o_ref):
    for q in range(N_GRP):
        o_ref[q, :, 0:GRP] = w_ref[:, q * GRP:(q + 1) * GRP].astype(BF16)
        o_ref[q, :, GRP:2 * GRP] = w_ref[:, C_CONV + q * GRP:C_CONV + (q + 1) * GRP].astype(BF16)
    for hh in range(N_GRP // 2):
        o_ref[N_GRP + hh] = w_ref[:, 2 * C_CONV + hh * 2 * GRP:2 * C_CONV + (hh + 1) * 2 * GRP].astype(BF16)


def _regroup_win(w_in):
    rows = 256
    return pl.pallas_call(
        _regroup_win_kernel,
        grid=(D_MODEL // rows,),
        in_specs=[pl.BlockSpec((rows, 2 * C_CONV + C_POOL), lambda i: (i, 0))],
        out_specs=pl.BlockSpec((N_WIN, rows, 2 * GRP), lambda i: (0, i, 0)),
        out_shape=jax.ShapeDtypeStruct((N_WIN, D_MODEL, 2 * GRP), BF16),
        compiler_params=pltpu.CompilerParams(dimension_semantics=("arbitrary",)),
        name="regroup_win",
    )(w_in)


def _mixer_prompt_kernel(x_ref, meta_ref, gmix_ref, wag_ref, wdw_ref, bdw_ref, lng_ref, lnb_ref,
                         wpool_ref, pscale_ref, wout_ref,
                         x1_ref, utail_ref, xbtail_ref,
                         ubuf, xbuf, umeta, xbmeta, wb, dbuf, cp):
    b = pl.program_id(0)
    t = pl.program_id(1)
    T = T_MIX

    def project(h, u_dst, u_row0, xb_dst, xb_row0):
        def put(dst, j0, row0, v):
            for i in range(v.shape[1] // LANES):
                dst[j0 + i, row0:row0 + v.shape[0], :] = v[:, _lane_tile(i)]
        for q in range(N_GRP):
            put(u_dst, q * TILES_PER_GRP, u_row0, _glu(h, wag_ref[q]))
        for hh in range(N_GRP // 2):
            xb = jnp.dot(h, wag_ref[N_GRP + hh], preferred_element_type=F32)
            put(xb_dst, hh * 2 * TILES_PER_GRP, xb_row0, xb)

    @pl.when((b == 0) & (t == 0))
    def _():
        project(_rmsnorm(meta_ref[...], gmix_ref[...]).astype(BF16), umeta, 0, xbmeta, 0)
        for k in range(CONV_WIDTH):
            wb[k] = jnp.broadcast_to(wdw_ref[k:k + 1, :], (SUBLANES, C_CONV))

    @pl.when(t == 0)
    def _():
        ubuf[:, 0:CONV_HALO - N_META, :] = jnp.zeros((N_CTILE, CONV_HALO - N_META, LANES), F32)
        ubuf[:, CONV_HALO - N_META:CONV_HALO, :] = umeta[...]
        xbuf[:, 0:POOL_HALO, :] = xbmeta[...]

    x = x_ref[0]
    project(_rmsnorm(x, gmix_ref[...]).astype(BF16), ubuf, CONV_HALO, xbuf, POOL_HALO)

    for j in range(N_CTILE):
        lanes = _lane_tile(j)
        w = POOL_WINDOWS[j // TILES_PER_GRP]
        for r0 in range(0, T, R_CONV):
            acc = jnp.broadcast_to(bdw_ref[:, lanes], (R_CONV, LANES))
            for k in range(CONV_WIDTH):
                row = r0 + (CONV_HALO - CONV_HIST) + k
                acc = acc + ubuf[j, row:row + R_CONV, :] * jnp.tile(wb[k, :, lanes], (R_CONV // SUBLANES, 1))
            cp[r0:r0 + R_CONV, lanes] = _head_ln_silu(acc, lng_ref[:, lanes], lnb_ref[:, lanes]).astype(BF16)
            cur = xbuf[j, POOL_HALO + r0:POOL_HALO + r0 + R_CONV, :]
            s = cur
            for d in range(1, w):
                s = s + xbuf[j, POOL_HALO + r0 - d:POOL_HALO + r0 - d + R_CONV, :]
            dbuf[r0:r0 + R_CONV, lanes] = (s * (1.0 / w) - cur).astype(BF16)

    for q in range(N_GRP):
        gsl = slice(q * GRP, (q + 1) * GRP)
        p = jnp.dot(dbuf[:, gsl], wpool_ref[q], preferred_element_type=F32) * pscale_ref[:, gsl]
        cp[:, C_CONV + q * GRP:C_CONV + (q + 1) * GRP] = p.astype(BF16)

    x1_ref[0] = x + jnp.dot(cp[...], wout_ref[...], preferred_element_type=F32)

    @pl.when(t == pl.num_programs(1) - 1)
    def _():
        for j in range(N_CTILE):
            utail_ref[0, :, _lane_tile(j)] = ubuf[j, T:T + CONV_HALO, :]
            xbtail_ref[0, :, _lane_tile(j)] = xbuf[j, T:T + POOL_HALO, :]

    ubuf[:, 0:CONV_HALO, :] = ubuf[:, T:T + CONV_HALO, :]
    xbuf[:, 0:POOL_HALO, :] = xbuf[:, T:T + POOL_HALO, :]


def _const_spec(shape):
    return pl.BlockSpec(shape, lambda *_: (0,) * len(shape), pipeline_mode=pl.Buffered(1))


def _mixer_weight_specs():
    return [
        _const_spec((1, D_MODEL)),
        _const_spec((N_WIN, D_MODEL, 2 * GRP)),
        _const_spec((CONV_WIDTH, C_CONV)),
        _const_spec((1, C_CONV)),
        _const_spec((1, C_CONV)),
        _const_spec((1, C_CONV)),
        _const_spec((N_GRP, GRP, GRP)),
        _const_spec((1, C_POOL)),
        _const_spec((C_CONV + C_POOL, D_MODEL)),
    ]


def _mixer_prompt(x_prompt, meta, mix_w):
    nb, seq, _ = x_prompt.shape
    T = T_MIX
    return pl.pallas_call(
        _mixer_prompt_kernel,
        grid=(nb, seq // T),
        in_specs=[
            pl.BlockSpec((1, T, D_MODEL), lambda b, t: (b, t, 0)),
            _const_spec((N_META, D_MODEL)),
        ] + _mixer_weight_specs(),
        out_specs=[
            pl.BlockSpec((1, T, D_MODEL), lambda b, t: (b, t, 0)),
            pl.BlockSpec((1, CONV_HALO, C_CONV), lambda b, t: (b, 0, 0)),
            pl.BlockSpec((1, POOL_HALO, C_POOL), lambda b, t: (b, 0, 0)),
        ],
        out_shape=[
            jax.ShapeDtypeStruct((nb, seq, D_MODEL), F32),
            jax.ShapeDtypeStruct((nb, CONV_HALO, C_CONV), F32),
            jax.ShapeDtypeStruct((nb, POOL_HALO, C_POOL), F32),
        ],
        scratch_shapes=[
            pltpu.VMEM((N_CTILE, CONV_HALO + T, LANES), F32),
            pltpu.VMEM((N_CTILE, POOL_HALO + T, LANES), F32),
            pltpu.VMEM((N_CTILE, N_META, LANES), F32),
            pltpu.VMEM((N_CTILE, N_META, LANES), F32),
            pltpu.VMEM((CONV_WIDTH, SUBLANES, C_CONV), F32),
            pltpu.VMEM((T, C_POOL), BF16),
            pltpu.VMEM((T, C_CONV + C_POOL), BF16),
        ],
        compiler_params=pltpu.CompilerParams(
            dimension_semantics=("arbitrary", "arbitrary"), vmem_limit_bytes=VMEM_LIMIT),
        name="mixer_prompt",
    )(x_prompt, meta, *mix_w)


def _mixer_sample_kernel(xs_ref, hist_ref, phist_ref, gmix_ref, wag_ref, wdw_ref, bdw_ref, lng_ref, lnb_ref,
                         wpool_ref, pscale_ref, wout_ref,
                         x1_ref, newc_ref, newp_ref, cp):
    S = S_SMP
    nj = xs_ref.shape[0]
    n_keep_c = CONV_HIST - nj
    n_keep_p = POOL_HIST - nj
    x = jnp.concatenate([xs_ref[j] for j in range(nj)], axis=0)
    h = _rmsnorm(x, gmix_ref[...]).astype(BF16)
    newc_ref[0:n_keep_c] = hist_ref[nj:CONV_HIST]
    newp_ref[0:n_keep_p] = phist_ref[nj:POOL_HIST]
    for q in range(N_GRP):
        u = _glu(h, wag_ref[q])
        for j in range(nj):
            newc_ref[n_keep_c + j, :, q * GRP:(q + 1) * GRP] = u[j * S:(j + 1) * S]
    for hh in range(N_GRP // 2):
        xb = jnp.dot(h, wag_ref[N_GRP + hh], preferred_element_type=F32)
        for j in range(nj):
            newp_ref[n_keep_p + j, :, hh * 2 * GRP:(hh + 1) * 2 * GRP] = xb[j * S:(j + 1) * S]

    def conv_tile(lt, carry):
        lane0 = pl.multiple_of(lt * LANES, LANES)
        lanes = pl.ds(lane0, LANES)
        bias = jnp.broadcast_to(bdw_ref[:, lanes], (S, LANES))
        acc = [bias for _ in range(nj)]
        for kk in range(CONV_HIST + nj):
            row = hist_ref[kk, :, lanes] if kk < CONV_HIST else newc_ref[kk - nj, :, lanes]
            for j in range(nj):
                k = kk - j
                if 0 <= k < CONV_WIDTH:
                    acc[j] = acc[j] + row * wdw_ref[k:k + 1, lanes]
        for j in range(nj):
            c = _head_ln_silu(acc[j], lng_ref[:, lanes], lnb_ref[:, lanes])
            cp[pl.ds(j * S, S), lanes] = c.astype(BF16)
        return carry

    lax.fori_loop(0, N_CTILE, conv_tile, 0)

    for gi, w in enumerate(POOL_WINDOWS):
        sl = slice(gi * GRP, (gi + 1) * GRP)
        rows = ([phist_ref[i, :, sl] for i in range(POOL_HIST - w + 1, POOL_HIST)]
                + [newp_ref[n_keep_p + j, :, sl] for j in range(nj)])
        ds = []
        for j in range(nj):
            win_rows = rows[j:j + w]
            s = win_rows[0]
            for r in win_rows[1:]:
                s = s + r
            ds.append(s * (1.0 / w) - win_rows[-1])
        d = jnp.concatenate(ds, axis=0).astype(BF16)
        p = jnp.dot(d, wpool_ref[gi], preferred_element_type=F32) * pscale_ref[:, sl]
        cp[:, C_CONV + gi * GRP:C_CONV + (gi + 1) * GRP] = p.astype(BF16)

    y = jnp.dot(cp[...], wout_ref[...], preferred_element_type=F32)
    x1 = x + y
    for j in range(nj):
        x1_ref[j] = x1[j * S:(j + 1) * S]


def _mixer_sample(xs_t, hist, phist, mix_w):
    nj, ns, _ = xs_t.shape
    S = S_SMP
    return pl.pallas_call(
        _mixer_sample_kernel,
        grid=(ns // S,),
        in_specs=[
            pl.BlockSpec((nj, S, D_MODEL), lambda s: (0, s, 0)),
            pl.BlockSpec((CONV_HIST, S, C_CONV), lambda s: (0, s, 0)),
            pl.BlockSpec((POOL_HIST, S, C_POOL), lambda s: (0, s, 0)),
        ] + _mixer_weight_specs(),
        out_specs=[
            pl.BlockSpec((nj, S, D_MODEL), lambda s: (0, s, 0)),
            pl.BlockSpec((CONV_HIST, S, C_CONV), lambda s: (0, s, 0)),
            pl.BlockSpec((POOL_HIST, S, C_POOL), lambda s: (0, s, 0)),
        ],
        out_shape=[
            jax.ShapeDtypeStruct((nj, ns, D_MODEL), F32),
            jax.ShapeDtypeStruct((CONV_HIST, ns, C_CONV), F32),
            jax.ShapeDtypeStruct((POOL_HIST, ns, C_POOL), F32),
        ],
        scratch_shapes=[pltpu.VMEM((nj * S, C_CONV + C_POOL), BF16)],
        compiler_params=pltpu.CompilerParams(
            dimension_semantics=("arbitrary",), vmem_limit_bytes=VMEM_LIMIT),
        name="mixer_sample",
    )(xs_t, hist, phist, *mix_w)


def _mlp_kernel(x_ref, gffn_ref, wup_ref, wdown_ref, gfin_ref, o_ref, h_ref):
    j = pl.program_id(1)
    last = pl.num_programs(1) - 1

    def ffn_chunk(h, acc):
        a = jnp.maximum(jnp.dot(h, wup_ref[...].astype(BF16), preferred_element_type=F32), 0.0)
        return acc + jnp.dot((a * a).astype(BF16), wdown_ref[...].astype(BF16), preferred_element_type=F32)

    @pl.when(j == 0)
    def _():
        x = x_ref[...]
        h = _rmsnorm(x, gffn_ref[...]).astype(BF16)
        h_ref[...] = h
        o_ref[...] = ffn_chunk(h, x)

    @pl.when((j > 0) & (j < last))
    def _():
        o_ref[...] = ffn_chunk(h_ref[...], o_ref[...])

    @pl.when(j == last)
    def _():
        o_ref[...] = _rmsnorm(ffn_chunk(h_ref[...], o_ref[...]), gfin_ref[...])


def _mlp(x1, gffn, wup, wdown, gfin):
    n, _ = x1.shape
    tm = min(TM_MLP, n)
    tf = TF_MLP * (TM_MLP // tm)
    return pl.pallas_call(
        _mlp_kernel,
        grid=(n // tm, D_FF // tf),
        in_specs=[
            pl.BlockSpec((tm, D_MODEL), lambda i, j: (i, 0)),
            _const_spec((1, D_MODEL)),
            pl.BlockSpec((D_MODEL, tf), lambda i, j: (0, j)),
            pl.BlockSpec((tf, D_MODEL), lambda i, j: (j, 0)),
            _const_spec((1, D_MODEL)),
        ],
        out_specs=pl.BlockSpec((tm, D_MODEL), lambda i, j: (i, 0)),
        out_shape=jax.ShapeDtypeStruct((n, D_MODEL), F32),
        scratch_shapes=[pltpu.VMEM((tm, D_MODEL), BF16)],
        compiler_params=pltpu.CompilerParams(
            dimension_semantics=("arbitrary", "arbitrary"), vmem_limit_bytes=VMEM_LIMIT),
        name="mlp",
    )(x1, gffn, wup, wdown, gfin)


def kernel(x_prompt, x_sample, state_conv, state_pool, meta_tokens, norm_mix_g, w_in, w_dw, b_dw, conv_ln_g, conv_ln_b, w_pool, pool_scale, w_out, norm_ffn_g, w_up, w_down, final_norm_g):
    assert norm_mix_g.shape[0] == 1, "single-layer step"
    nb, seq, _ = x_prompt.shape
    ns, nj, _ = x_sample.shape

    gmix = norm_mix_g[0][None]
    gffn = norm_ffn_g[0][None]
    gfin = final_norm_g[None]
    mix_w = (gmix, _regroup_win(w_in[0]), w_dw[0], b_dw[0][None],
             conv_ln_g[0][None], conv_ln_b[0][None], w_pool[0].astype(BF16), pool_scale[0][None], w_out[0].astype(BF16))

    x1p, utail, xbtail = _mixer_prompt(x_prompt, meta_tokens, mix_w)
    y_prompt = _mlp(x1p.reshape(nb * seq, D_MODEL), gffn, w_up[0], w_down[0], gfin).reshape(nb, seq, D_MODEL)
    new_conv_prompt = utail[:, CONV_HALO - CONV_HIST:][None]
    new_pool_prompt = xbtail[:, POOL_HALO - POOL_HIST:][None]

    xs_t = jnp.transpose(x_sample, (1, 0, 2))
    hist_t = jnp.transpose(state_conv[0], (1, 0, 2))
    phist_t = jnp.transpose(state_pool[0], (1, 0, 2))
    x1s_t, newc_t, newp_t = _mixer_sample(xs_t, hist_t, phist_t, mix_w)
    ys_t = _mlp(x1s_t.reshape(nj * ns, D_MODEL), gffn, w_up[0], w_down[0], gfin).reshape(nj, ns, D_MODEL)
    y_sample = jnp.transpose(ys_t, (1, 0, 2))
    new_conv_sample = jnp.transpose(newc_t, (1, 0, 2))[None]
    new_pool_sample = jnp.transpose(newp_t, (1, 0, 2))[None]
    return (y_prompt, y_sample, new_conv_prompt, new_pool_prompt, new_conv_sample, new_pool_sample)
```

```python
import jax
import jax.numpy as jnp
from jax import lax
from jax.experimental import pallas as pl
from jax.experimental.pallas import tpu as pltpu

D_MODEL = 2048
C_CONV = 1024
C_POOL = 1024
CONV_WIDTH = 31
CONV_HIST = CONV_WIDTH - 1
POOL_WINDOWS = (2, 4, 8, 16)
N_GRP = len(POOL_WINDOWS)
GRP = C_POOL // N_GRP
POOL_HIST = max(POOL_WINDOWS) - 1
D_FF = 4 * D_MODEL
N_META = 16
EPS = 1e-6

LANES = 128
SUBLANES = 8
N_CTILE = C_CONV // LANES
TILES_PER_GRP = GRP // LANES
CONV_HALO = 32
POOL_HALO = 16

N_WIN = N_GRP + N_GRP // 2
T_MIX = 512
R_CONV = 64
S_SMP = 32
TM_MLP = 1024
TF_MLP = 512
VMEM_BYTES = 64 * 1024 * 1024
VMEM_LIMIT = VMEM_BYTES - 3 * 1024 * 1024

F32 = jnp.float32
BF16 = jnp.bfloat16


def _rmsnorm(x, g):
    return x * lax.rsqrt(jnp.mean(x * x, axis=-1, keepdims=True) + EPS) * g


def _glu(h, wag_q):
    proj = jnp.dot(h, wag_q, preferred_element_type=F32)
    return proj[:, :GRP] * jax.nn.sigmoid(proj[:, GRP:])


def _head_ln_silu(c, g, b):
    mu = jnp.mean(c, axis=-1, keepdims=True)
    d = c - mu
    var = jnp.mean(d * d, axis=-1, keepdims=True)
    y = d * lax.rsqrt(var + EPS) * g + b
    return y * jax.nn.sigmoid(y)


def _lane_tile(j):
    return slice(j * LANES, (j + 1) * LANES)


def _regroup_win_kernel(w_ref, o_ref):
    for q in range(N_GRP):
        o_ref[q, :, 0:GRP] = w_ref[:, q * GRP:(q + 1) * GRP].astype(BF16)
        o_ref[q, :, GRP:2 * GRP] = w_ref[:, C_CONV + q * GRP:C_CONV + (q + 1) * GRP].astype(BF16)
    for hh in range(N_GRP // 2):
        o_ref[N_GRP + hh] = w_ref[:, 2 * C_CONV + hh * 2 * GRP:2 * C_CONV + (hh + 1) * 2 * GRP].astype(BF16)


def _regroup_win(w_in):
    rows = 256
    return pl.pallas_call(
        _regroup_win_kernel,
        grid=(D_MODEL // rows,),
        in_specs=[pl.BlockSpec((rows, 2 * C_CONV + C_POOL), lambda i: (i, 0))],
        out_specs=pl.BlockSpec((N_WIN, rows, 2 * GRP), lambda i: (0, i, 0)),
        out_shape=jax.ShapeDtypeStruct((N_WIN, D_MODEL, 2 * GRP), BF16),
        compiler_params=pltpu.CompilerParams(dimension_semantics=("arbitrary",)),
        name="regroup_win",
    )(w_in)


def _mixer_prompt_kernel(x_ref, meta_ref, gmix_ref, wag_ref, wdw_ref, bdw_ref, lng_ref, lnb_ref,
                         wpool_ref, pscale_ref, wout_ref,
                         x1_ref, utail_ref, xbtail_ref,
                         ubuf, xbuf, umeta, xbmeta, wb, dbuf, cp):
    b = pl.program_id(0)
    t = pl.program_id(1)
    T = T_MIX

    def project(h, u_dst, u_row0, xb_dst, xb_row0):
        def put(dst, j0, row0, v):
            for i in range(v.shape[1] // LANES):
                dst[j0 + i, row0:row0 + v.shape[0], :] = v[:, _lane_tile(i)]
        for q in range(N_GRP):
            put(u_dst, q * TILES_PER_GRP, u_row0, _glu(h, wag_ref[q]))
        for hh in range(N_GRP // 2):
            xb = jnp.dot(h, wag_ref[N_GRP + hh], preferred_element_type=F32)
            put(xb_dst, hh * 2 * TILES_PER_GRP, xb_row0, xb)

    @pl.when((b == 0) & (t == 0))
    def _():
        project(_rmsnorm(meta_ref[...], gmix_ref[...]).astype(BF16), umeta, 0, xbmeta, 0)
        for k in range(CONV_WIDTH):
            wb[k] = jnp.broadcast_to(wdw_ref[k:k + 1, :], (SUBLANES, C_CONV))

    @pl.when(t == 0)
    def _():
        ubuf[:, 0:CONV_HALO - N_META, :] = jnp.zeros((N_CTILE, CONV_HALO - N_META, LANES), F32)
        ubuf[:, CONV_HALO - N_META:CONV_HALO, :] = umeta[...]
        xbuf[:, 0:POOL_HALO, :] = xbmeta[...]

    x = x_ref[0]
    project(_rmsnorm(x, gmix_ref[...]).astype(BF16), ubuf, CONV_HALO, xbuf, POOL_HALO)

    for j in range(N_CTILE):
        lanes = _lane_tile(j)
        w = POOL_WINDOWS[j // TILES_PER_GRP]
        for r0 in range(0, T, R_CONV):
            acc = jnp.broadcast_to(bdw_ref[:, lanes], (R_CONV, LANES))
            for k in range(CONV_WIDTH):
                row = r0 + (CONV_HALO - CONV_HIST) + k
                acc = acc + ubuf[j, row:row + R_CONV, :] * jnp.tile(wb[k, :, lanes], (R_CONV // SUBLANES, 1))
            cp[r0:r0 + R_CONV, lanes] = _head_ln_silu(acc, lng_ref[:, lanes], lnb_ref[:, lanes]).astype(BF16)
            cur = xbuf[j, POOL_HALO + r0:POOL_HALO + r0 + R_CONV, :]
            s = cur
            for d in range(1, w):
                s = s + xbuf[j, POOL_HALO + r0 - d:POOL_HALO + r0 - d + R_CONV, :]
            dbuf[r0:r0 + R_CONV, lanes] = (s * (1.0 / w) - cur).astype(BF16)

    for q in range(N_GRP):
        gsl = slice(q * GRP, (q + 1) * GRP)
        p = jnp.dot(dbuf[:, gsl], wpool_ref[q], preferred_element_type=F32) * pscale_ref[:, gsl]
        cp[:, C_CONV + q * GRP:C_CONV + (q + 1) * GRP] = p.astype(BF16)

    x1_ref[0] = x + jnp.dot(cp[...], wout_ref[...], preferred_element_type=F32)

    @pl.when(t == pl.num_programs(1) - 1)
    def _():
        for j in range(N_CTILE):
            utail_ref[0, :, _lane_tile(j)] = ubuf[j, T:T + CONV_HALO, :]
            xbtail_ref[0, :, _lane_tile(j)] = xbuf[j, T:T + POOL_HALO, :]

    ubuf[:, 0:CONV_HALO, :] = ubuf[:, T:T + CONV_HALO, :]
    xbuf[:, 0:POOL_HALO, :] = xbuf[:, T:T + POOL_HALO, :]


def _const_spec(shape):
    return pl.BlockSpec(shape, lambda *_: (0,) * len(shape), pipeline_mode=pl.Buffered(1))


def _mixer_weight_specs():
    return [
        _const_spec((1, D_MODEL)),
        _const_spec((N_WIN, D_MODEL, 2 * GRP)),
        _const_spec((CONV_WIDTH, C_CONV)),
        _const_spec((1, C_CONV)),
        _const_spec((1, C_CONV)),
        _const_spec((1, C_CONV)),
        _const_spec((N_GRP, GRP, GRP)),
        _const_spec((1, C_POOL)),
        _const_spec((C_CONV + C_POOL, D_MODEL)),
    ]


def _mixer_prompt(x_prompt, meta, mix_w):
    nb, seq, _ = x_prompt.shape
    T = T_MIX
    return pl.pallas_call(
        _mixer_prompt_kernel,
        grid=(nb, seq // T),
        in_specs=[
            pl.BlockSpec((1, T, D_MODEL), lambda b, t: (b, t, 0)),
            _const_spec((N_META, D_MODEL)),
        ] + _mixer_weight_specs(),
        out_specs=[
            pl.BlockSpec((1, T, D_MODEL), lambda b, t: (b, t, 0)),
            pl.BlockSpec((1, CONV_HALO, C_CONV), lambda b, t: (b, 0, 0)),
            pl.BlockSpec((1, POOL_HALO, C_POOL), lambda b, t: (b, 0, 0)),
        ],
        out_shape=[
            jax.ShapeDtypeStruct((nb, seq, D_MODEL), F32),
            jax.ShapeDtypeStruct((nb, CONV_HALO, C_CONV), F32),
            jax.ShapeDtypeStruct((nb, POOL_HALO, C_POOL), F32),
        ],
        scratch_shapes=[
            pltpu.VMEM((N_CTILE, CONV_HALO + T, LANES), F32),
            pltpu.VMEM((N_CTILE, POOL_HALO + T, LANES), F32),
            pltpu.VMEM((N_CTILE, N_META, LANES), F32),
            pltpu.VMEM((N_CTILE, N_META, LANES), F32),
            pltpu.VMEM((CONV_WIDTH, SUBLANES, C_CONV), F32),
            pltpu.VMEM((T, C_POOL), BF16),
            pltpu.VMEM((T, C_CONV + C_POOL), BF16),
        ],
        compiler_params=pltpu.CompilerParams(
            dimension_semantics=("arbitrary", "arbitrary"), vmem_limit_bytes=VMEM_LIMIT),
        name="mixer_prompt",
    )(x_prompt, meta, *mix_w)


def _mixer_sample_kernel(xs_ref, hist_ref, phist_ref, gmix_ref, wag_ref, wdw_ref, bdw_ref, lng_ref, lnb_ref,
                         wpool_ref, pscale_ref, wout_ref,
                         x1_ref, newc_ref, newp_ref, cp):
    S = S_SMP
    nj = xs_ref.shape[0]
    n_keep_c = CONV_HIST - nj
    n_keep_p = POOL_HIST - nj
    x = jnp.concatenate([xs_ref[j] for j in range(nj)], axis=0)
    h = _rmsnorm(x, gmix_ref[...]).astype(BF16)
    newc_ref[0:n_keep_c] = hist_ref[nj:CONV_HIST]
    newp_ref[0:n_keep_p] = phist_ref[nj:POOL_HIST]
    for q in range(N_GRP):
        u = _glu(h, wag_ref[q])
        for j in range(nj):
            newc_ref[n_keep_c + j, :, q * GRP:(q + 1) * GRP] = u[j * S:(j + 1) * S]
    for hh in range(N_GRP // 2):
        xb = jnp.dot(h, wag_ref[N_GRP + hh], preferred_element_type=F32)
        for j in range(nj):
            newp_ref[n_keep_p + j, :, hh * 2 * GRP:(hh + 1) * 2 * GRP] = xb[j * S:(j + 1) * S]

    def conv_tile(lt, carry):
        lane0 = pl.multiple_of(lt * LANES, LANES)
        lanes = pl.ds(lane0, LANES)
        bias = jnp.broadcast_to(bdw_ref[:, lanes], (S, LANES))
        acc = [bias for _ in range(nj)]
        for kk in range(CONV_HIST + nj):
            row = hist_ref[kk, :, lanes] if kk < CONV_HIST else newc_ref[kk - nj, :, lanes]
            for j in range(nj):
                k = kk - j
                if 0 <= k < CONV_WIDTH:
                    acc[j] = acc[j] + row * wdw_ref[k:k + 1, lanes]
        for j in range(nj):
            c = _head_ln_silu(acc[j], lng_ref[:, lanes], lnb_ref[:, lanes])
            cp[pl.ds(j * S, S), lanes] = c.astype(BF16)
        return carry

    lax.fori_loop(0, N_CTILE, conv_tile, 0)

    for gi, w in enumerate(POOL_WINDOWS):
        sl = slice(gi * GRP, (gi + 1) * GRP)
        rows = ([phist_ref[i, :, sl] for i in range(POOL_HIST - w + 1, POOL_HIST)]
                + [newp_ref[n_keep_p + j, :, sl] for j in range(nj)])
        ds = []
        for j in range(nj):
            win_rows = rows[j:j + w]
            s = win_rows[0]
            for r in win_rows[1:]:
                s = s + r
            ds.append(s * (1.0 / w) - win_rows[-1])
        d = jnp.concatenate(ds, axis=0).astype(BF16)
        p = jnp.dot(d, wpool_ref[gi], preferred_element_type=F32) * pscale_ref[:, sl]
        cp[:, C_CONV + gi * GRP:C_CONV + (gi + 1) * GRP] = p.astype(BF16)

    y = jnp.dot(cp[...], wout_ref[...], preferred_element_type=F32)
    x1 = x + y
    for j in range(nj):
        x1_ref[j] = x1[j * S:(j + 1) * S]


def _mixer_sample(xs_t, hist, phist, mix_w):
    nj, ns, _ = xs_t.shape
    S = S_SMP
    return pl.pallas_call(
        _mixer_sample_kernel,
        grid=(ns // S,),
        in_specs=[
            pl.BlockSpec((nj, S, D_MODEL), lambda s: (0, s, 0)),
            pl.BlockSpec((CONV_HIST, S, C_CONV), lambda s: (0, s, 0)),
            pl.BlockSpec((POOL_HIST, S, C_POOL), lambda s: (0, s, 0)),
        ] + _mixer_weight_specs(),
        out_specs=[
            pl.BlockSpec((nj, S, D_MODEL), lambda s: (0, s, 0)),
            pl.BlockSpec((CONV_HIST, S, C_CONV), lambda s: (0, s, 0)),
            pl.BlockSpec((POOL_HIST, S, C_POOL), lambda s: (0, s, 0)),
        ],
        out_shape=[
            jax.ShapeDtypeStruct((nj, ns, D_MODEL), F32),
            jax.ShapeDtypeStruct((CONV_HIST, ns, C_CONV), F32),
            jax.ShapeDtypeStruct((POOL_HIST, ns, C_POOL), F32),
        ],
        scratch_shapes=[pltpu.VMEM((nj * S, C_CONV + C_POOL), BF16)],
        compiler_params=pltpu.CompilerParams(
            dimension_semantics=("arbitrary",), vmem_limit_bytes=VMEM_LIMIT),
        name="mixer_sample",
    )(xs_t, hist, phist, *mix_w)


def _mlp_kernel(x_ref, gffn_ref, wup_ref, wdown_ref, gfin_ref, o_ref, h_ref):
    j = pl.program_id(1)
    last = pl.num_programs(1) - 1

    def ffn_chunk(h, acc):
        a = jnp.maximum(jnp.dot(h, wup_ref[...].astype(BF16), preferred_element_type=F32), 0.0)
        return acc + jnp.dot((a * a).astype(BF16), wdown_ref[...].astype(BF16), preferred_element_type=F32)

    @pl.when(j == 0)
    def _():
        x = x_ref[...]
        h = _rmsnorm(x, gffn_ref[...]).astype(BF16)
        h_ref[...] = h
        o_ref[...] = ffn_chunk(h, x)

    @pl.when((j > 0) & (j < last))
    def _():
        o_ref[...] = ffn_chunk(h_ref[...], o_ref[...])

    @pl.when(j == last)
    def _():
        o_ref[...] = _rmsnorm(ffn_chunk(h_ref[...], o_ref[...]), gfin_ref[...])


def _mlp(x1, gffn, wup, wdown, gfin):
    n, _ = x1.shape
    tm = min(TM_MLP, n)
    tf = TF_MLP * (TM_MLP // tm)
    return pl.pallas_call(
        _mlp_kernel,
        grid=(n // tm, D_FF // tf),
        in_specs=[
            pl.BlockSpec((tm, D_MODEL), lambda i, j: (i, 0)),
            _const_spec((1, D_MODEL)),
            pl.BlockSpec((D_MODEL, tf), lambda i, j: (0, j)),
            pl.BlockSpec((tf, D_MODEL), lambda i, j: (j, 0)),
            _const_spec((1, D_MODEL)),
        ],
        out_specs=pl.BlockSpec((tm, D_MODEL), lambda i, j: (i, 0)),
        out_shape=jax.ShapeDtypeStruct((n, D_MODEL), F32),
        scratch_shapes=[pltpu.VMEM((tm, D_MODEL), BF16)],
        compiler_params=pltpu.CompilerParams(
            dimension_semantics=("arbitrary", "arbitrary"), vmem_limit_bytes=VMEM_LIMIT),
        name="mlp",
    )(x1, gffn, wup, wdown, gfin)


def kernel(x_prompt, x_sample, state_conv, state_pool, meta_tokens, norm_mix_g, w_in, w_dw, b_dw, conv_ln_g, conv_ln_b, w_pool, pool_scale, w_out, norm_ffn_g, w_up, w_down, final_norm_g):
    assert norm_mix_g.shape[0] == 1, "single-layer step"
    nb, seq, _ = x_prompt.shape
    ns, nj, _ = x_sample.shape

    gmix = norm_mix_g[0][None]
    gffn = norm_ffn_g[0][None]
    gfin = final_norm_g[None]
    mix_w = (gmix, _regroup_win(w_in[0]), w_dw[0], b_dw[0][None],
             conv_ln_g[0][None], conv_ln_b[0][None], w_pool[0].astype(BF16), pool_scale[0][None], w_out[0].astype(BF16))

    x1p, utail, xbtail = _mixer_prompt(x_prompt, meta_tokens, mix_w)
    y_prompt = _mlp(x1p.reshape(nb * seq, D_MODEL), gffn, w_up[0], w_down[0], gfin).reshape(nb, seq, D_MODEL)
    new_conv_prompt = utail[:, CONV_HALO - CONV_HIST:][None]
    new_pool_prompt = xbtail[:, POOL_HALO - POOL_HIST:][None]

    xs_t = jnp.transpose(x_sample, (1, 0, 2))
    hist_t = jnp.transpose(state_conv[0], (1, 0, 2))
    phist_t = jnp.transpose(state_pool[0], (1, 0, 2))
    x1s_t, newc_t, newp_t = _mixer_sample(xs_t, hist_t, phist_t, mix_w)
    ys_t = _mlp(x1s_t.reshape(nj * ns, D_MODEL), gffn, w_up[0], w_down[0], gfin).reshape(nj, ns, D_MODEL)
    y_sample = jnp.transpose(ys_t, (1, 0, 2))
    new_conv_sample = jnp.transpose(newc_t, (1, 0, 2))[None]
    new_pool_sample = jnp.transpose(newp_t, (1, 0, 2))[None]
    return (y_prompt, y_sample, new_conv_prompt, new_pool_prompt, new_conv_sample, new_pool_sample)
```

```python
import jax
import jax.numpy as jnp
from jax import lax
from jax.experimental import pallas as pl
from jax.experimental.pallas import tpu as pltpu

D_MODEL = 2048
C_CONV = 1024
C_POOL = 1024
CONV_WIDTH = 31
CONV_HIST = CONV_WIDTH - 1
POOL_WINDOWS = (2, 4, 8, 16)
N_GRP = len(POOL_WINDOWS)
GRP = C_POOL // N_GRP
POOL_HIST = max(POOL_WINDOWS) - 1
D_FF = 4 * D_MODEL
N_META = 16
EPS = 1e-6

LANES = 128
SUBLANES = 8
N_CTILE = C_CONV // LANES
TILES_PER_GRP = GRP // LANES
CONV_HALO = 32
POOL_HALO = 16

N_WIN = N_GRP + N_GRP // 2
T_MIX = 512
R_CONV = 64
S_SMP = 32
TM_MLP = 1024
TF_MLP = 512
VMEM_BYTES = 64 * 1024 * 1024
VMEM_LIMIT = VMEM_BYTES - 1 * 1024 * 1024

F32 = jnp.float32
BF16 = jnp.bfloat16


def _rmsnorm(x, g):
    return x * lax.rsqrt(jnp.mean(x * x, axis=-1, keepdims=True) + EPS) * g


def _glu(h, wag_q):
    proj = jnp.dot(h, wag_q, preferred_element_type=F32)
    return proj[:, :GRP] * jax.nn.sigmoid(proj[:, GRP:])


def _head_ln_silu(c, g, b):
    mu = jnp.mean(c, axis=-1, keepdims=True)
    d = c - mu
    var = jnp.mean(d * d, axis=-1, keepdims=True)
    y = d * lax.rsqrt(var + EPS) * g + b
    return y * jax.nn.sigmoid(y)


def _lane_tile(j):
    return slice(j * LANES, (j + 1) * LANES)


def _regroup_win_kernel(w_ref, o_ref):
    for q in range(N_GRP):
        o_ref[q, :, 0:GRP] = w_ref[:, q * GRP:(q + 1) * GRP].astype(BF16)
        o_ref[q, :, GRP:2 * GRP] = w_ref[:, C_CONV + q * GRP:C_CONV + (q + 1) * GRP].astype(BF16)
    for hh in range(N_GRP // 2):
        o_ref[N_GRP + hh] = w_ref[:, 2 * C_CONV + hh * 2 * GRP:2 * C_CONV + (hh + 1) * 2 * GRP].astype(BF16)


def _regroup_win(w_in):
    rows = 256
    return pl.pallas_call(
        _regroup_win_kernel,
        grid=(D_MODEL // rows,),
        in_specs=[pl.BlockSpec((rows, 2 * C_CONV + C_POOL), lambda i: (i, 0))],
        out_specs=pl.BlockSpec((N_WIN, rows, 2 * GRP), lambda i: (0, i, 0)),
        out_shape=jax.ShapeDtypeStruct((N_WIN, D_MODEL, 2 * GRP), BF16),
        compiler_params=pltpu.CompilerParams(dimension_semantics=("arbitrary",)),
        name="regroup_win",
    )(w_in)


def _mixer_prompt_kernel(x_ref, meta_ref, gmix_ref, wag_ref, wdw_ref, bdw_ref, lng_ref, lnb_ref,
                         wpool_ref, pscale_ref, wout_ref,
                         x1_ref, utail_ref, xbtail_ref,
                         ubuf, xbuf, umeta, xbmeta, wb, dbuf, cp):
    b = pl.program_id(0)
    t = pl.program_id(1)
    T = T_MIX

    def project(h, u_dst, u_row0, xb_dst, xb_row0):
        def put(dst, j0, row0, v):
            for i in range(v.shape[1] // LANES):
                dst[j0 + i, row0:row0 + v.shape[0], :] = v[:, _lane_tile(i)]
        for q in range(N_GRP):
            put(u_dst, q * TILES_PER_GRP, u_row0, _glu(h, wag_ref[q]))
        for hh in range(N_GRP // 2):
            xb = jnp.dot(h, wag_ref[N_GRP + hh], preferred_element_type=F32)
            put(xb_dst, hh * 2 * TILES_PER_GRP, xb_row0, xb)

    @pl.when((b == 0) & (t == 0))
    def _():
        project(_rmsnorm(meta_ref[...], gmix_ref[...]).astype(BF16), umeta, 0, xbmeta, 0)
        for k in range(CONV_WIDTH):
            wb[k] = jnp.broadcast_to(wdw_ref[k:k + 1, :], (SUBLANES, C_CONV))

    @pl.when(t == 0)
    def _():
        ubuf[:, 0:CONV_HALO - N_META, :] = jnp.zeros((N_CTILE, CONV_HALO - N_META, LANES), F32)
        ubuf[:, CONV_HALO - N_META:CONV_HALO, :] = umeta[...]
        xbuf[:, 0:POOL_HALO, :] = xbmeta[...]

    x = x_ref[0]
    project(_rmsnorm(x, gmix_ref[...]).astype(BF16), ubuf, CONV_HALO, xbuf, POOL_HALO)

    for j in range(N_CTILE):
        lanes = _lane_tile(j)
        w = POOL_WINDOWS[j // TILES_PER_GRP]
        for r0 in range(0, T, R_CONV):
            acc = jnp.broadcast_to(bdw_ref[:, lanes], (R_CONV, LANES))
            for k in range(CONV_WIDTH):
                row = r0 + (CONV_HALO - CONV_HIST) + k
                acc = acc + ubuf[j, row:row + R_CONV, :] * jnp.tile(wb[k, :, lanes], (R_CONV // SUBLANES, 1))
            cp[r0:r0 + R_CONV, lanes] = _head_ln_silu(acc, lng_ref[:, lanes], lnb_ref[:, lanes]).astype(BF16)
            cur = xbuf[j, POOL_HALO + r0:POOL_HALO + r0 + R_CONV, :]
            s = cur
            for d in range(1, w):
                s = s + xbuf[j, POOL_HALO + r0 - d:POOL_HALO + r0 - d + R_CONV, :]
            dbuf[r0:r0 + R_CONV, lanes] = (s * (1.0 / w) - cur).astype(BF16)

    for q in range(N_GRP):
        gsl = slice(q * GRP, (q + 1) * GRP)
        p = jnp.dot(dbuf[:, gsl], wpool_ref[q], preferred_element_type=F32) * pscale_ref[:, gsl]
        cp[:, C_CONV + q * GRP:C_CONV + (q + 1) * GRP] = p.astype(BF16)

    x1_ref[0] = x + jnp.dot(cp[...], wout_ref[...], preferred_element_type=F32)

    @pl.when(t == pl.num_programs(1) - 1)
    def _():
        for j in range(N_CTILE):
            utail_ref[0, :, _lane_tile(j)] = ubuf[j, T:T + CONV_HALO, :]
            xbtail_ref[0, :, _lane_tile(j)] = xbuf[j, T:T + POOL_HALO, :]

    ubuf[:, 0:CONV_HALO, :] = ubuf[:, T:T + CONV_HALO, :]
    xbuf[:, 0:POOL_HALO, :] = xbuf[:, T:T + POOL_HALO, :]


def _const_spec(shape):
    return pl.BlockSpec(shape, lambda *_: (0,) * len(shape), pipeline_mode=pl.Buffered(1))


def _mixer_weight_specs():
    return [
        _const_spec((1, D_MODEL)),
        _const_spec((N_WIN, D_MODEL, 2 * GRP)),
        _const_spec((CONV_WIDTH, C_CONV)),
        _const_spec((1, C_CONV)),
        _const_spec((1, C_CONV)),
        _const_spec((1, C_CONV)),
        _const_spec((N_GRP, GRP, GRP)),
        _const_spec((1, C_POOL)),
        _const_spec((C_CONV + C_POOL, D_MODEL)),
    ]


def _mixer_prompt(x_prompt, meta, mix_w):
    nb, seq, _ = x_prompt.shape
    T = T_MIX
    return pl.pallas_call(
        _mixer_prompt_kernel,
        grid=(nb, seq // T),
        in_specs=[
            pl.BlockSpec((1, T, D_MODEL), lambda b, t: (b, t, 0)),
            _const_spec((N_META, D_MODEL)),
        ] + _mixer_weight_specs(),
        out_specs=[
            pl.BlockSpec((1, T, D_MODEL), lambda b, t: (b, t, 0)),
            pl.BlockSpec((1, CONV_HALO, C_CONV), lambda b, t: (b, 0, 0)),
            pl.BlockSpec((1, POOL_HALO, C_POOL), lambda b, t: (b, 0, 0)),
        ],
        out_shape=[
            jax.ShapeDtypeStruct((nb, seq, D_MODEL), F32),
            jax.ShapeDtypeStruct((nb, CONV_HALO, C_CONV), F32),
            jax.ShapeDtypeStruct((nb, POOL_HALO, C_POOL), F32),
        ],
        scratch_shapes=[
            pltpu.VMEM((N_CTILE, CONV_HALO + T, LANES), F32),
            pltpu.VMEM((N_CTILE, POOL_HALO + T, LANES), F32),
            pltpu.VMEM((N_CTILE, N_META, LANES), F32),
            pltpu.VMEM((N_CTILE, N_META, LANES), F32),
            pltpu.VMEM((CONV_WIDTH, SUBLANES, C_CONV), F32),
            pltpu.VMEM((T, C_POOL), BF16),
            pltpu.VMEM((T, C_CONV + C_POOL), BF16),
        ],
        compiler_params=pltpu.CompilerParams(
            dimension_semantics=("arbitrary", "arbitrary"), vmem_limit_bytes=VMEM_LIMIT),
        name="mixer_prompt",
    )(x_prompt, meta, *mix_w)


def _mixer_sample_kernel(xs_ref, hist_ref, phist_ref, gmix_ref, wag_ref, wdw_ref, bdw_ref, lng_ref, lnb_ref,
                         wpool_ref, pscale_ref, wout_ref,
                         x1_ref, newc_ref, newp_ref, cp):
    S = S_SMP
    nj = xs_ref.shape[0]
    n_keep_c = CONV_HIST - nj
    n_keep_p = POOL_HIST - nj
    x = jnp.concatenate([xs_ref[j] for j in range(nj)], axis=0)
    h = _rmsnorm(x, gmix_ref[...]).astype(BF16)
    newc_ref[0:n_keep_c] = hist_ref[nj:CONV_HIST]
    newp_ref[0:n_keep_p] = phist_ref[nj:POOL_HIST]
    for q in range(N_GRP):
        u = _glu(h, wag_ref[q])
        for j in range(nj):
            newc_ref[n_keep_c + j, :, q * GRP:(q + 1) * GRP] = u[j * S:(j + 1) * S]
    for hh in range(N_GRP // 2):
        xb = jnp.dot(h, wag_ref[N_GRP + hh], preferred_element_type=F32)
        for j in range(nj):
            newp_ref[n_keep_p + j, :, hh * 2 * GRP:(hh + 1) * 2 * GRP] = xb[j * S:(j + 1) * S]

    for lt in range(N_CTILE):
        lanes = _lane_tile(lt)
        bias = jnp.broadcast_to(bdw_ref[:, lanes], (S, LANES))
        acc = [bias for _ in range(nj)]
        for kk in range(CONV_HIST + nj):
            row = hist_ref[kk, :, lanes] if kk < CONV_HIST else newc_ref[kk - nj, :, lanes]
            for j in range(nj):
                k = kk - j
                if 0 <= k < CONV_WIDTH:
                    acc[j] = acc[j] + row * wdw_ref[k:k + 1, lanes]
        for j in range(nj):
            c = _head_ln_silu(acc[j], lng_ref[:, lanes], lnb_ref[:, lanes])
            cp[j * S:(j + 1) * S, lanes] = c.astype(BF16)

    for gi, w in enumerate(POOL_WINDOWS):
        sl = slice(gi * GRP, (gi + 1) * GRP)
        rows = ([phist_ref[i, :, sl] for i in range(POOL_HIST - w + 1, POOL_HIST)]
                + [newp_ref[n_keep_p + j, :, sl] for j in range(nj)])
        ds = []
        for j in range(nj):
            win_rows = rows[j:j + w]
            s = win_rows[0]
            for r in win_rows[1:]:
                s = s + r
            ds.append(s * (1.0 / w) - win_rows[-1])
        d = jnp.concatenate(ds, axis=0).astype(BF16)
        p = jnp.dot(d, wpool_ref[gi], preferred_element_type=F32) * pscale_ref[:, sl]
        cp[:, C_CONV + gi * GRP:C_CONV + (gi + 1) * GRP] = p.astype(BF16)

    y = jnp.dot(cp[...], wout_ref[...], preferred_element_type=F32)
    x1 = x + y
    for j in range(nj):
        x1_ref[j] = x1[j * S:(j + 1) * S]


def _mixer_sample(xs_t, hist, phist, mix_w):
    nj, ns, _ = xs_t.shape
    S = S_SMP
    return pl.pallas_call(
        _mixer_sample_kernel,
        grid=(ns // S,),
        in_specs=[
            pl.BlockSpec((nj, S, D_MODEL), lambda s: (0, s, 0)),
            pl.BlockSpec((CONV_HIST, S, C_CONV), lambda s: (0, s, 0)),
            pl.BlockSpec((POOL_HIST, S, C_POOL), lambda s: (0, s, 0)),
        ] + _mixer_weight_specs(),
        out_specs=[
            pl.BlockSpec((nj, S, D_MODEL), lambda s: (0, s, 0)),
            pl.BlockSpec((CONV_HIST, S, C_CONV), lambda s: (0, s, 0)),
            pl.BlockSpec((POOL_HIST, S, C_POOL), lambda s: (0, s, 0)),
        ],
        out_shape=[
            jax.ShapeDtypeStruct((nj, ns, D_MODEL), F32),
            jax.ShapeDtypeStruct((CONV_HIST, ns, C_CONV), F32),
            jax.ShapeDtypeStruct((POOL_HIST, ns, C_POOL), F32),
        ],
        scratch_shapes=[pltpu.VMEM((nj * S, C_CONV + C_POOL), BF16)],
        compiler_params=pltpu.CompilerParams(
            dimension_semantics=("arbitrary",), vmem_limit_bytes=VMEM_LIMIT),
        name="mixer_sample",
    )(xs_t, hist, phist, *mix_w)


def _mlp_kernel(xp_ref, xs_ref, gffn_ref, wup_ref, wdown_ref, gfin_ref, op_ref, os_ref, h_ref):
    j = pl.program_id(1)
    last = pl.num_programs(1) - 1
    tp = xp_ref.shape[0]

    def ffn_chunk(h):
        a = jnp.maximum(jnp.dot(h, wup_ref[...].astype(BF16), preferred_element_type=F32), 0.0)
        return jnp.dot((a * a).astype(BF16), wdown_ref[...].astype(BF16), preferred_element_type=F32)

    @pl.when(j == 0)
    def _():
        xp = xp_ref[...]
        xs = xs_ref[...]
        h = jnp.concatenate([_rmsnorm(xp, gffn_ref[...]), _rmsnorm(xs, gffn_ref[...])], axis=0).astype(BF16)
        h_ref[...] = h
        d = ffn_chunk(h)
        op_ref[...] = xp + d[:tp]
        os_ref[...] = xs + d[tp:]

    @pl.when((j > 0) & (j < last))
    def _():
        d = ffn_chunk(h_ref[...])
        op_ref[...] += d[:tp]
        os_ref[...] += d[tp:]

    @pl.when(j == last)
    def _():
        d = ffn_chunk(h_ref[...])
        op_ref[...] = _rmsnorm(op_ref[...] + d[:tp], gfin_ref[...])
        os_ref[...] = _rmsnorm(os_ref[...] + d[tp:], gfin_ref[...])


def _mlp(x1p, x1s, gffn, wup, wdown, gfin):
    n_p, _ = x1p.shape
    n_s, _ = x1s.shape
    n_tiles = n_p // TM_MLP
    ts = n_s // n_tiles
    return pl.pallas_call(
        _mlp_kernel,
        grid=(n_tiles, D_FF // TF_MLP),
        in_specs=[
            pl.BlockSpec((TM_MLP, D_MODEL), lambda i, j: (i, 0)),
            pl.BlockSpec((ts, D_MODEL), lambda i, j: (i, 0)),
            _const_spec((1, D_MODEL)),
            pl.BlockSpec((D_MODEL, TF_MLP), lambda i, j: (0, j)),
            pl.BlockSpec((TF_MLP, D_MODEL), lambda i, j: (j, 0)),
            _const_spec((1, D_MODEL)),
        ],
        out_specs=[
            pl.BlockSpec((TM_MLP, D_MODEL), lambda i, j: (i, 0)),
            pl.BlockSpec((ts, D_MODEL), lambda i, j: (i, 0)),
        ],
        out_shape=[
            jax.ShapeDtypeStruct((n_p, D_MODEL), F32),
            jax.ShapeDtypeStruct((n_s, D_MODEL), F32),
        ],
        scratch_shapes=[pltpu.VMEM((TM_MLP + ts, D_MODEL), BF16)],
        compiler_params=pltpu.CompilerParams(
            dimension_semantics=("arbitrary", "arbitrary"), vmem_limit_bytes=VMEM_LIMIT),
        name="mlp",
    )(x1p, x1s, gffn, wup, wdown, gfin)


def kernel(x_prompt, x_sample, state_conv, state_pool, meta_tokens, norm_mix_g, w_in, w_dw, b_dw, conv_ln_g, conv_ln_b, w_pool, pool_scale, w_out, norm_ffn_g, w_up, w_down, final_norm_g):
    assert norm_mix_g.shape[0] == 1, "single-layer step"
    nb, seq, _ = x_prompt.shape
    ns, nj, _ = x_sample.shape

    gmix = norm_mix_g[0][None]
    gffn = norm_ffn_g[0][None]
    gfin = final_norm_g[None]
    mix_w = (gmix, _regroup_win(w_in[0]), w_dw[0], b_dw[0][None],
             conv_ln_g[0][None], conv_ln_b[0][None], w_pool[0].astype(BF16), pool_scale[0][None], w_out[0].astype(BF16))

    x1p, utail, xbtail = _mixer_prompt(x_prompt, meta_tokens, mix_w)
    new_conv_prompt = utail[:, CONV_HALO - CONV_HIST:][None]
    new_pool_prompt = xbtail[:, POOL_HALO - POOL_HIST:][None]

    xs_t = jnp.transpose(x_sample, (1, 0, 2))
    hist_t = jnp.transpose(state_conv[0], (1, 0, 2))
    phist_t = jnp.transpose(state_pool[0], (1, 0, 2))
    x1s_t, newc_t, newp_t = _mixer_sample(xs_t, hist_t, phist_t, mix_w)

    yp, ys = _mlp(x1p.reshape(nb * seq, D_MODEL), x1s_t.reshape(nj * ns, D_MODEL), gffn, w_up[0], w_down[0], gfin)
    y_prompt = yp.reshape(nb, seq, D_MODEL)
    y_sample = jnp.transpose(ys.reshape(nj, ns, D_MODEL), (1, 0, 2))
    new_conv_sample = jnp.transpose(newc_t, (1, 0, 2))[None]
    new_pool_sample = jnp.transpose(newp_t, (1, 0, 2))[None]
    return (y_prompt, y_sample, new_conv_prompt, new_pool_prompt, new_conv_sample, new_pool_sample)
```

```python
import jax
import jax.numpy as jnp
from jax import lax
from jax.experimental import pallas as pl
from jax.experimental.pallas import tpu as pltpu

D_MODEL = 2048
C_CONV = 1024
C_POOL = 1024
CONV_WIDTH = 31
CONV_HIST = CONV_WIDTH - 1
POOL_WINDOWS = (2, 4, 8, 16)
N_GRP = len(POOL_WINDOWS)
GRP = C_POOL // N_GRP
POOL_HIST = max(POOL_WINDOWS) - 1
D_FF = 4 * D_MODEL
N_META = 16
EPS = 1e-6

LANES = 128
SUBLANES = 8
N_CTILE = C_CONV // LANES
TILES_PER_GRP = GRP // LANES
CONV_HALO = 32
POOL_HALO = 16

N_WIN = N_GRP + N_GRP // 2
T_MIX = 512
R_CONV = 64
S_SMP = 32
TM_MLP = 1024
TF_MLP = 512
VMEM_BYTES = 64 * 1024 * 1024
VMEM_LIMIT = VMEM_BYTES - 1 * 1024 * 1024

F32 = jnp.float32
BF16 = jnp.bfloat16


def _rmsnorm(x, g):
    return x * lax.rsqrt(jnp.mean(x * x, axis=-1, keepdims=True) + EPS) * g


def _glu(h, wag_q):
    proj = jnp.dot(h, wag_q, preferred_element_type=F32)
    return proj[:, :GRP] * jax.nn.sigmoid(proj[:, GRP:])


def _head_ln_silu(c, g, b):
    mu = jnp.mean(c, axis=-1, keepdims=True)
    d = c - mu
    var = jnp.mean(d * d, axis=-1, keepdims=True)
    y = d * lax.rsqrt(var + EPS) * g + b
    return y * jax.nn.sigmoid(y)


def _lane_tile(j):
    return slice(j * LANES, (j + 1) * LANES)


def _regroup_win_kernel(w_ref, o_ref):
    for q in range(N_GRP):
        o_ref[q, :, 0:GRP] = w_ref[:, q * GRP:(q + 1) * GRP].astype(BF16)
        o_ref[q, :, GRP:2 * GRP] = w_ref[:, C_CONV + q * GRP:C_CONV + (q + 1) * GRP].astype(BF16)
    for hh in range(N_GRP // 2):
        o_ref[N_GRP + hh] = w_ref[:, 2 * C_CONV + hh * 2 * GRP:2 * C_CONV + (hh + 1) * 2 * GRP].astype(BF16)


def _regroup_win(w_in):
    rows = 512
    return pl.pallas_call(
        _regroup_win_kernel,
        grid=(D_MODEL // rows,),
        in_specs=[pl.BlockSpec((rows, 2 * C_CONV + C_POOL), lambda i: (i, 0))],
        out_specs=pl.BlockSpec((N_WIN, rows, 2 * GRP), lambda i: (0, i, 0)),
        out_shape=jax.ShapeDtypeStruct((N_WIN, D_MODEL, 2 * GRP), BF16),
        compiler_params=pltpu.CompilerParams(dimension_semantics=("arbitrary",)),
        name="regroup_win",
    )(w_in)


def _mixer_prompt_kernel(x_ref, meta_ref, gmix_ref, wag_ref, wdw_ref, bdw_ref, lng_ref, lnb_ref,
                         wpool_ref, pscale_ref, wout_ref,
                         x1_ref, utail_ref, xbtail_ref,
                         ubuf, xbuf, umeta, xbmeta, wb, dbuf, cp):
    b = pl.program_id(0)
    t = pl.program_id(1)
    T = T_MIX

    def project(h, u_dst, u_row0, xb_dst, xb_row0):
        def put(dst, j0, row0, v):
            for i in range(v.shape[1] // LANES):
                dst[j0 + i, row0:row0 + v.shape[0], :] = v[:, _lane_tile(i)]
        for q in range(N_GRP):
            put(u_dst, q * TILES_PER_GRP, u_row0, _glu(h, wag_ref[q]))
        for hh in range(N_GRP // 2):
            xb = jnp.dot(h, wag_ref[N_GRP + hh], preferred_element_type=F32)
            put(xb_dst, hh * 2 * TILES_PER_GRP, xb_row0, xb)

    @pl.when((b == 0) & (t == 0))
    def _():
        project(_rmsnorm(meta_ref[...], gmix_ref[...]).astype(BF16), umeta, 0, xbmeta, 0)
        for k in range(CONV_WIDTH):
            wb[k] = jnp.broadcast_to(wdw_ref[k:k + 1, :], (SUBLANES, C_CONV))

    @pl.when(t == 0)
    def _():
        ubuf[:, 0:CONV_HALO - N_META, :] = jnp.zeros((N_CTILE, CONV_HALO - N_META, LANES), F32)
        ubuf[:, CONV_HALO - N_META:CONV_HALO, :] = umeta[...]
        xbuf[:, 0:POOL_HALO, :] = xbmeta[...]

    x = x_ref[0]
    project(_rmsnorm(x, gmix_ref[...]).astype(BF16), ubuf, CONV_HALO, xbuf, POOL_HALO)

    for j in range(N_CTILE):
        lanes = _lane_tile(j)
        w = POOL_WINDOWS[j // TILES_PER_GRP]
        for r0 in range(0, T, R_CONV):
            acc = jnp.broadcast_to(bdw_ref[:, lanes], (R_CONV, LANES))
            for k in range(CONV_WIDTH):
                row = r0 + (CONV_HALO - CONV_HIST) + k
                acc = acc + ubuf[j, row:row + R_CONV, :] * jnp.tile(wb[k, :, lanes], (R_CONV // SUBLANES, 1))
            cp[r0:r0 + R_CONV, lanes] = _head_ln_silu(acc, lng_ref[:, lanes], lnb_ref[:, lanes]).astype(BF16)
            cur = xbuf[j, POOL_HALO + r0:POOL_HALO + r0 + R_CONV, :]
            s = cur
            for d in range(1, w):
                s = s + xbuf[j, POOL_HALO + r0 - d:POOL_HALO + r0 - d + R_CONV, :]
            dbuf[r0:r0 + R_CONV, lanes] = (s * (1.0 / w) - cur).astype(BF16)

    for q in range(N_GRP):
        gsl = slice(q * GRP, (q + 1) * GRP)
        p = jnp.dot(dbuf[:, gsl], wpool_ref[q], preferred_element_type=F32) * pscale_ref[:, gsl]
        cp[:, C_CONV + q * GRP:C_CONV + (q + 1) * GRP] = p.astype(BF16)

    x1_ref[0] = x + jnp.dot(cp[...], wout_ref[...], preferred_element_type=F32)

    @pl.when(t == pl.num_programs(1) - 1)
    def _():
        for j in range(N_CTILE):
            utail_ref[0, :, _lane_tile(j)] = ubuf[j, T:T + CONV_HALO, :]
            xbtail_ref[0, :, _lane_tile(j)] = xbuf[j, T:T + POOL_HALO, :]

    ubuf[:, 0:CONV_HALO, :] = ubuf[:, T:T + CONV_HALO, :]
    xbuf[:, 0:POOL_HALO, :] = xbuf[:, T:T + POOL_HALO, :]


def _const_spec(shape):
    return pl.BlockSpec(shape, lambda *_: (0,) * len(shape), pipeline_mode=pl.Buffered(1))


def _mixer_weight_specs():
    return [
        _const_spec((1, D_MODEL)),
        _const_spec((N_WIN, D_MODEL, 2 * GRP)),
        _const_spec((CONV_WIDTH, C_CONV)),
        _const_spec((1, C_CONV)),
        _const_spec((1, C_CONV)),
        _const_spec((1, C_CONV)),
        _const_spec((N_GRP, GRP, GRP)),
        _const_spec((1, C_POOL)),
        _const_spec((C_CONV + C_POOL, D_MODEL)),
    ]


def _mixer_prompt(x_prompt, meta, mix_w):
    nb, seq, _ = x_prompt.shape
    T = T_MIX
    assert seq % T == 0 and seq >= CONV_HALO and meta.shape[0] == N_META, (seq, meta.shape)
    return pl.pallas_call(
        _mixer_prompt_kernel,
        grid=(nb, seq // T),
        in_specs=[
            pl.BlockSpec((1, T, D_MODEL), lambda b, t: (b, t, 0)),
            _const_spec((N_META, D_MODEL)),
        ] + _mixer_weight_specs(),
        out_specs=[
            pl.BlockSpec((1, T, D_MODEL), lambda b, t: (b, t, 0)),
            pl.BlockSpec((1, CONV_HALO, C_CONV), lambda b, t: (b, 0, 0)),
            pl.BlockSpec((1, POOL_HALO, C_POOL), lambda b, t: (b, 0, 0)),
        ],
        out_shape=[
            jax.ShapeDtypeStruct((nb, seq, D_MODEL), F32),
            jax.ShapeDtypeStruct((nb, CONV_HALO, C_CONV), F32),
            jax.ShapeDtypeStruct((nb, POOL_HALO, C_POOL), F32),
        ],
        scratch_shapes=[
            pltpu.VMEM((N_CTILE, CONV_HALO + T, LANES), F32),
            pltpu.VMEM((N_CTILE, POOL_HALO + T, LANES), F32),
            pltpu.VMEM((N_CTILE, N_META, LANES), F32),
            pltpu.VMEM((N_CTILE, N_META, LANES), F32),
            pltpu.VMEM((CONV_WIDTH, SUBLANES, C_CONV), F32),
            pltpu.VMEM((T, C_POOL), BF16),
            pltpu.VMEM((T, C_CONV + C_POOL), BF16),
        ],
        compiler_params=pltpu.CompilerParams(
            dimension_semantics=("arbitrary", "arbitrary"), vmem_limit_bytes=VMEM_LIMIT),
        name="mixer_prompt",
    )(x_prompt, meta, *mix_w)


def _mixer_sample_kernel(xs_ref, hist_ref, phist_ref, gmix_ref, wag_ref, wdw_ref, bdw_ref, lng_ref, lnb_ref,
                         wpool_ref, pscale_ref, wout_ref,
                         x1_ref, newc_ref, newp_ref, cp):
    S = S_SMP
    nj = xs_ref.shape[0]
    n_keep_c = CONV_HIST - nj
    n_keep_p = POOL_HIST - nj
    x = jnp.concatenate([xs_ref[j] for j in range(nj)], axis=0)
    h = _rmsnorm(x, gmix_ref[...]).astype(BF16)
    newc_ref[0:n_keep_c] = hist_ref[nj:CONV_HIST]
    newp_ref[0:n_keep_p] = phist_ref[nj:POOL_HIST]
    for q in range(N_GRP):
        u = _glu(h, wag_ref[q])
        for j in range(nj):
            newc_ref[n_keep_c + j, :, q * GRP:(q + 1) * GRP] = u[j * S:(j + 1) * S]
    for hh in range(N_GRP // 2):
        xb = jnp.dot(h, wag_ref[N_GRP + hh], preferred_element_type=F32)
        for j in range(nj):
            newp_ref[n_keep_p + j, :, hh * 2 * GRP:(hh + 1) * 2 * GRP] = xb[j * S:(j + 1) * S]

    for lt in range(N_CTILE):
        lanes = _lane_tile(lt)
        bias = jnp.broadcast_to(bdw_ref[:, lanes], (S, LANES))
        acc = [bias for _ in range(nj)]
        for kk in range(CONV_HIST + nj):
            row = hist_ref[kk, :, lanes] if kk < CONV_HIST else newc_ref[kk - nj, :, lanes]
            for j in range(nj):
                k = kk - j
                if 0 <= k < CONV_WIDTH:
                    acc[j] = acc[j] + row * wdw_ref[k:k + 1, lanes]
        for j in range(nj):
            c = _head_ln_silu(acc[j], lng_ref[:, lanes], lnb_ref[:, lanes])
            cp[j * S:(j + 1) * S, lanes] = c.astype(BF16)

    for gi, w in enumerate(POOL_WINDOWS):
        sl = slice(gi * GRP, (gi + 1) * GRP)
        rows = ([phist_ref[i, :, sl] for i in range(POOL_HIST - w + 1, POOL_HIST)]
                + [newp_ref[n_keep_p + j, :, sl] for j in range(nj)])
        ds = []
        for j in range(nj):
            win_rows = rows[j:j + w]
            s = win_rows[0]
            for r in win_rows[1:]:
                s = s + r
            ds.append(s * (1.0 / w) - win_rows[-1])
        d = jnp.concatenate(ds, axis=0).astype(BF16)
        p = jnp.dot(d, wpool_ref[gi], preferred_element_type=F32) * pscale_ref[:, sl]
        cp[:, C_CONV + gi * GRP:C_CONV + (gi + 1) * GRP] = p.astype(BF16)

    y = jnp.dot(cp[...], wout_ref[...], preferred_element_type=F32)
    x1 = x + y
    for j in range(nj):
        x1_ref[j] = x1[j * S:(j + 1) * S]


def _mixer_sample(xs_t, hist, phist, mix_w):
    nj, ns, _ = xs_t.shape
    S = S_SMP
    assert ns % S == 0 and hist.shape[0] == CONV_HIST and phist.shape[0] == POOL_HIST, (ns, hist.shape, phist.shape)
    return pl.pallas_call(
        _mixer_sample_kernel,
        grid=(ns // S,),
        in_specs=[
            pl.BlockSpec((nj, S, D_MODEL), lambda s: (0, s, 0)),
            pl.BlockSpec((CONV_HIST, S, C_CONV), lambda s: (0, s, 0)),
            pl.BlockSpec((POOL_HIST, S, C_POOL), lambda s: (0, s, 0)),
        ] + _mixer_weight_specs(),
        out_specs=[
            pl.BlockSpec((nj, S, D_MODEL), lambda s: (0, s, 0)),
            pl.BlockSpec((CONV_HIST, S, C_CONV), lambda s: (0, s, 0)),
            pl.BlockSpec((POOL_HIST, S, C_POOL), lambda s: (0, s, 0)),
        ],
        out_shape=[
            jax.ShapeDtypeStruct((nj, ns, D_MODEL), F32),
            jax.ShapeDtypeStruct((CONV_HIST, ns, C_CONV), F32),
            jax.ShapeDtypeStruct((POOL_HIST, ns, C_POOL), F32),
        ],
        scratch_shapes=[pltpu.VMEM((nj * S, C_CONV + C_POOL), BF16)],
        compiler_params=pltpu.CompilerParams(
            dimension_semantics=("arbitrary",), vmem_limit_bytes=VMEM_LIMIT),
        name="mixer_sample",
    )(xs_t, hist, phist, *mix_w)


def _mlp_kernel(xp_ref, xs_ref, gffn_ref, wup_ref, wdown_ref, gfin_ref, op_ref, os_ref, h_ref):
    j = pl.program_id(1)
    last = pl.num_programs(1) - 1
    tp = xp_ref.shape[0]

    def ffn_chunk(h):
        a = jnp.maximum(jnp.dot(h, wup_ref[...].astype(BF16), preferred_element_type=F32), 0.0)
        return jnp.dot((a * a).astype(BF16), wdown_ref[...].astype(BF16), preferred_element_type=F32)

    @pl.when(j == 0)
    def _():
        xp = xp_ref[...]
        xs = xs_ref[...]
        h = jnp.concatenate([_rmsnorm(xp, gffn_ref[...]), _rmsnorm(xs, gffn_ref[...])], axis=0).astype(BF16)
        h_ref[...] = h
        d = ffn_chunk(h)
        op_ref[...] = xp + d[:tp]
        os_ref[...] = xs + d[tp:]

    @pl.when((j > 0) & (j < last))
    def _():
        d = ffn_chunk(h_ref[...])
        op_ref[...] += d[:tp]
        os_ref[...] += d[tp:]

    @pl.when(j == last)
    def _():
        d = ffn_chunk(h_ref[...])
        op_ref[...] = _rmsnorm(op_ref[...] + d[:tp], gfin_ref[...])
        os_ref[...] = _rmsnorm(os_ref[...] + d[tp:], gfin_ref[...])


def _mlp(x1p, x1s, gffn, wup, wdown, gfin):
    n_p, _ = x1p.shape
    n_s, _ = x1s.shape
    n_tiles = n_p // TM_MLP
    ts = n_s // n_tiles
    assert n_p == n_tiles * TM_MLP and n_s == n_tiles * ts and ts % (2 * SUBLANES) == 0, (n_p, n_s)
    return pl.pallas_call(
        _mlp_kernel,
        grid=(n_tiles, D_FF // TF_MLP),
        in_specs=[
            pl.BlockSpec((TM_MLP, D_MODEL), lambda i, j: (i, 0)),
            pl.BlockSpec((ts, D_MODEL), lambda i, j: (i, 0)),
            _const_spec((1, D_MODEL)),
            pl.BlockSpec((D_MODEL, TF_MLP), lambda i, j: (0, j)),
            pl.BlockSpec((TF_MLP, D_MODEL), lambda i, j: (j, 0)),
            _const_spec((1, D_MODEL)),
        ],
        out_specs=[
            pl.BlockSpec((TM_MLP, D_MODEL), lambda i, j: (i, 0)),
            pl.BlockSpec((ts, D_MODEL), lambda i, j: (i, 0)),
        ],
        out_shape=[
            jax.ShapeDtypeStruct((n_p, D_MODEL), F32),
            jax.ShapeDtypeStruct((n_s, D_MODEL), F32),
        ],
        scratch_shapes=[pltpu.VMEM((TM_MLP + ts, D_MODEL), BF16)],
        compiler_params=pltpu.CompilerParams(
            dimension_semantics=("arbitrary", "arbitrary"), vmem_limit_bytes=VMEM_LIMIT),
        name="mlp",
    )(x1p, x1s, gffn, wup, wdown, gfin)


def kernel(x_prompt, x_sample, state_conv, state_pool, meta_tokens, norm_mix_g, w_in, w_dw, b_dw, conv_ln_g, conv_ln_b, w_pool, pool_scale, w_out, norm_ffn_g, w_up, w_down, final_norm_g):
    assert norm_mix_g.shape[0] == 1, "single-layer step"
    nb, seq, _ = x_prompt.shape
    ns, nj, _ = x_sample.shape

    gmix = norm_mix_g[0][None]
    gffn = norm_ffn_g[0][None]
    gfin = final_norm_g[None]
    mix_w = (gmix, _regroup_win(w_in[0]), w_dw[0], b_dw[0][None],
             conv_ln_g[0][None], conv_ln_b[0][None], w_pool[0].astype(BF16), pool_scale[0][None], w_out[0].astype(BF16))

    x1p, utail, xbtail = _mixer_prompt(x_prompt, meta_tokens, mix_w)
    new_conv_prompt = utail[:, CONV_HALO - CONV_HIST:][None]
    new_pool_prompt = xbtail[:, POOL_HALO - POOL_HIST:][None]

    xs_t = jnp.transpose(x_sample, (1, 0, 2))
    hist_t = jnp.transpose(state_conv[0], (1, 0, 2))
    phist_t = jnp.transpose(state_pool[0], (1, 0, 2))
    x1s_t, newc_t, newp_t = _mixer_sample(xs_t, hist_t, phist_t, mix_w)

    yp, ys = _mlp(x1p.reshape(nb * seq, D_MODEL), x1s_t.reshape(nj * ns, D_MODEL), gffn, w_up[0], w_down[0], gfin)
    y_prompt = yp.reshape(nb, seq, D_MODEL)
    y_sample = jnp.transpose(ys.reshape(nj, ns, D_MODEL), (1, 0, 2))
    new_conv_sample = jnp.transpose(newc_t, (1, 0, 2))[None]
    new_pool_sample = jnp.transpose(newp_t, (1, 0, 2))[None]
    return (y_prompt, y_sample, new_conv_prompt, new_pool_prompt, new_conv_sample, new_pool_sample)
```

```python
import jax
import jax.numpy as jnp
from jax import lax
from jax.experimental import pallas as pl
from jax.experimental.pallas import tpu as pltpu

D_MODEL = 2048
C_CONV = 1024
C_POOL = 1024
CONV_WIDTH = 31
CONV_HIST = CONV_WIDTH - 1
POOL_WINDOWS = (2, 4, 8, 16)
N_GRP = len(POOL_WINDOWS)
GRP = C_POOL // N_GRP
POOL_HIST = max(POOL_WINDOWS) - 1
D_FF = 4 * D_MODEL
N_META = 16
EPS = 1e-6

LANES = 128
SUBLANES = 8
N_CTILE = C_CONV // LANES
TILES_PER_GRP = GRP // LANES
CONV_HALO = 32
POOL_HALO = 16

N_WIN = N_GRP + N_GRP // 2
T_MIX = 256
R_CONV = 64
S_SMP = 32
TM_MLP = 1024
TF_MLP = 1024
VMEM_BYTES = 64 * 1024 * 1024
VMEM_LIMIT = VMEM_BYTES - 1 * 1024 * 1024

F32 = jnp.float32
BF16 = jnp.bfloat16


def _rmsnorm(x, g):
    return x * lax.rsqrt(jnp.mean(x * x, axis=-1, keepdims=True) + EPS) * g


def _glu(h, wag_q):
    proj = jnp.dot(h, wag_q, preferred_element_type=F32)
    return proj[:, :GRP] * jax.nn.sigmoid(proj[:, GRP:])


def _head_ln_silu(c, g, b):
    mu = jnp.mean(c, axis=-1, keepdims=True)
    d = c - mu
    var = jnp.mean(d * d, axis=-1, keepdims=True)
    y = d * lax.rsqrt(var + EPS) * g + b
    return y * jax.nn.sigmoid(y)


def _lane_tile(j):
    return slice(j * LANES, (j + 1) * LANES)


def _regroup_win_kernel(w_ref, wpool_ref, wout_ref, o_ref, wpoolb_ref, woutb_ref):
    for q in range(N_GRP):
        o_ref[q, :, 0:GRP] = w_ref[:, q * GRP:(q + 1) * GRP].astype(BF16)
        o_ref[q, :, GRP:2 * GRP] = w_ref[:, C_CONV + q * GRP:C_CONV + (q + 1) * GRP].astype(BF16)
    for hh in range(N_GRP // 2):
        o_ref[N_GRP + hh] = w_ref[:, 2 * C_CONV + hh * 2 * GRP:2 * C_CONV + (hh + 1) * 2 * GRP].astype(BF16)
    wpoolb_ref[...] = wpool_ref[...].astype(BF16)
    woutb_ref[...] = wout_ref[...].astype(BF16)


def _regroup_win(w_in, w_pool, w_out):
    rows = D_MODEL // N_GRP
    assert w_pool.shape == (N_GRP, GRP, GRP) and w_out.shape == (C_CONV + C_POOL, D_MODEL), (w_pool.shape, w_out.shape)
    rows_out = (C_CONV + C_POOL) // N_GRP
    return pl.pallas_call(
        _regroup_win_kernel,
        grid=(N_GRP,),
        in_specs=[
            pl.BlockSpec((rows, 2 * C_CONV + C_POOL), lambda i: (i, 0)),
            pl.BlockSpec((1, GRP, GRP), lambda i: (i, 0, 0)),
            pl.BlockSpec((rows_out, D_MODEL), lambda i: (i, 0)),
        ],
        out_specs=(
            pl.BlockSpec((N_WIN, rows, 2 * GRP), lambda i: (0, i, 0)),
            pl.BlockSpec((1, GRP, GRP), lambda i: (i, 0, 0)),
            pl.BlockSpec((rows_out, D_MODEL), lambda i: (i, 0)),
        ),
        out_shape=(
            jax.ShapeDtypeStruct((N_WIN, D_MODEL, 2 * GRP), BF16),
            jax.ShapeDtypeStruct((N_GRP, GRP, GRP), BF16),
            jax.ShapeDtypeStruct((C_CONV + C_POOL, D_MODEL), BF16),
        ),
        compiler_params=pltpu.CompilerParams(dimension_semantics=("arbitrary",)),
        name="regroup_win",
    )(w_in, w_pool, w_out)


def _mixer_prompt_kernel(x_ref, meta_ref, wup_ref, wdown_ref, gmix_ref, wag_ref, wdw_ref, bdw_ref, lng_ref, lnb_ref,
                         wpool_ref, pscale_ref, wout_ref,
                         x1_ref, utail_ref, xbtail_ref, wupb_ref, wdownb_ref,
                         ubuf, xbuf, umeta, xbmeta, wb, dbuf, cp):
    wupb_ref[...] = wup_ref[...].astype(BF16)
    wdownb_ref[...] = wdown_ref[...].astype(BF16)
    b = pl.program_id(0)
    t = pl.program_id(1)
    T = T_MIX

    def project(h, u_dst, u_row0, xb_dst, xb_row0):
        def put(dst, j0, row0, v):
            for i in range(v.shape[1] // LANES):
                dst[j0 + i, row0:row0 + v.shape[0], :] = v[:, _lane_tile(i)]
        for q in range(N_GRP):
            put(u_dst, q * TILES_PER_GRP, u_row0, _glu(h, wag_ref[q]))
        for hh in range(N_GRP // 2):
            xb = jnp.dot(h, wag_ref[N_GRP + hh], preferred_element_type=F32)
            put(xb_dst, hh * 2 * TILES_PER_GRP, xb_row0, xb)

    @pl.when((b == 0) & (t == 0))
    def _():
        project(_rmsnorm(meta_ref[...], gmix_ref[...]).astype(BF16), umeta, 0, xbmeta, 0)
        for k in range(CONV_WIDTH):
            wb[k] = jnp.broadcast_to(wdw_ref[k:k + 1, :], (SUBLANES, C_CONV))

    @pl.when(t == 0)
    def _():
        ubuf[:, 0:CONV_HALO - N_META, :] = jnp.zeros((N_CTILE, CONV_HALO - N_META, LANES), F32)
        ubuf[:, CONV_HALO - N_META:CONV_HALO, :] = umeta[...]
        xbuf[:, 0:POOL_HALO, :] = xbmeta[...]

    x = x_ref[0]
    project(_rmsnorm(x, gmix_ref[...]).astype(BF16), ubuf, CONV_HALO, xbuf, POOL_HALO)

    for j in range(N_CTILE):
        lanes = _lane_tile(j)
        w = POOL_WINDOWS[j // TILES_PER_GRP]
        for r0 in range(0, T, R_CONV):
            acc = jnp.broadcast_to(bdw_ref[:, lanes], (R_CONV, LANES))
            for k in range(CONV_WIDTH):
                row = r0 + (CONV_HALO - CONV_HIST) + k
                acc = acc + ubuf[j, row:row + R_CONV, :] * jnp.tile(wb[k, :, lanes], (R_CONV // SUBLANES, 1))
            cp[r0:r0 + R_CONV, lanes] = _head_ln_silu(acc, lng_ref[:, lanes], lnb_ref[:, lanes]).astype(BF16)
            cur = xbuf[j, POOL_HALO + r0:POOL_HALO + r0 + R_CONV, :]
            s = cur
            for d in range(1, w):
                s = s + xbuf[j, POOL_HALO + r0 - d:POOL_HALO + r0 - d + R_CONV, :]
            dbuf[r0:r0 + R_CONV, lanes] = (s * (1.0 / w) - cur).astype(BF16)

    for q in range(N_GRP):
        gsl = slice(q * GRP, (q + 1) * GRP)
        p = jnp.dot(dbuf[:, gsl], wpool_ref[q], preferred_element_type=F32) * pscale_ref[:, gsl]
        cp[:, C_CONV + q * GRP:C_CONV + (q + 1) * GRP] = p.astype(BF16)

    x1_ref[0] = x + jnp.dot(cp[...], wout_ref[...], preferred_element_type=F32)

    @pl.when(t == pl.num_programs(1) - 1)
    def _():
        for j in range(N_CTILE):
            utail_ref[0, :, _lane_tile(j)] = ubuf[j, T:T + CONV_HALO, :]
            xbtail_ref[0, :, _lane_tile(j)] = xbuf[j, T:T + POOL_HALO, :]

    ubuf[:, 0:CONV_HALO, :] = ubuf[:, T:T + CONV_HALO, :]
    xbuf[:, 0:POOL_HALO, :] = xbuf[:, T:T + POOL_HALO, :]


def _const_spec(shape):
    return pl.BlockSpec(shape, lambda *_: (0,) * len(shape), pipeline_mode=pl.Buffered(1))


def _mixer_weight_specs():
    return [
        _const_spec((1, D_MODEL)),
        _const_spec((N_WIN, D_MODEL, 2 * GRP)),
        _const_spec((CONV_WIDTH, C_CONV)),
        _const_spec((1, C_CONV)),
        _const_spec((1, C_CONV)),
        _const_spec((1, C_CONV)),
        _const_spec((N_GRP, GRP, GRP)),
        _const_spec((1, C_POOL)),
        _const_spec((C_CONV + C_POOL, D_MODEL)),
    ]


def _mixer_prompt(x_prompt, meta, wup, wdown, mix_w):
    nb, seq, _ = x_prompt.shape
    T = T_MIX
    nt = seq // T
    n_steps = nb * nt
    assert seq % T == 0 and seq >= CONV_HALO and meta.shape[0] == N_META, (seq, meta.shape)
    assert D_MODEL % (n_steps * 2 * SUBLANES) == 0 and D_FF % (n_steps * 2 * SUBLANES) == 0, n_steps
    ru, rd = D_MODEL // n_steps, D_FF // n_steps

    def step(b, t):
        return (b * nt + t, 0)

    return pl.pallas_call(
        _mixer_prompt_kernel,
        grid=(nb, nt),
        in_specs=[
            pl.BlockSpec((1, T, D_MODEL), lambda b, t: (b, t, 0)),
            _const_spec((N_META, D_MODEL)),
            pl.BlockSpec((ru, D_FF), step),
            pl.BlockSpec((rd, D_MODEL), step),
        ] + _mixer_weight_specs(),
        out_specs=[
            pl.BlockSpec((1, T, D_MODEL), lambda b, t: (b, t, 0)),
            pl.BlockSpec((1, CONV_HALO, C_CONV), lambda b, t: (b, 0, 0)),
            pl.BlockSpec((1, POOL_HALO, C_POOL), lambda b, t: (b, 0, 0)),
            pl.BlockSpec((ru, D_FF), step),
            pl.BlockSpec((rd, D_MODEL), step),
        ],
        out_shape=[
            jax.ShapeDtypeStruct((nb, seq, D_MODEL), F32),
            jax.ShapeDtypeStruct((nb, CONV_HALO, C_CONV), F32),
            jax.ShapeDtypeStruct((nb, POOL_HALO, C_POOL), F32),
            jax.ShapeDtypeStruct((D_MODEL, D_FF), BF16),
            jax.ShapeDtypeStruct((D_FF, D_MODEL), BF16),
        ],
        scratch_shapes=[
            pltpu.VMEM((N_CTILE, CONV_HALO + T, LANES), F32),
            pltpu.VMEM((N_CTILE, POOL_HALO + T, LANES), F32),
            pltpu.VMEM((N_CTILE, N_META, LANES), F32),
            pltpu.VMEM((N_CTILE, N_META, LANES), F32),
            pltpu.VMEM((CONV_WIDTH, SUBLANES, C_CONV), F32),
            pltpu.VMEM((T, C_POOL), BF16),
            pltpu.VMEM((T, C_CONV + C_POOL), BF16),
        ],
        compiler_params=pltpu.CompilerParams(
            dimension_semantics=("arbitrary", "arbitrary"), vmem_limit_bytes=VMEM_LIMIT),
        name="mixer_prompt",
    )(x_prompt, meta, wup, wdown, *mix_w)


def _mixer_sample_kernel(xs_ref, hist_ref, phist_ref, gmix_ref, wag_ref, wdw_ref, bdw_ref, lng_ref, lnb_ref,
                         wpool_ref, pscale_ref, wout_ref,
                         x1_ref, newc_ref, newp_ref, cp):
    S = S_SMP
    nj = xs_ref.shape[0]
    n_keep_c = CONV_HIST - nj
    n_keep_p = POOL_HIST - nj
    x = jnp.concatenate([xs_ref[j] for j in range(nj)], axis=0)
    h = _rmsnorm(x, gmix_ref[...]).astype(BF16)
    newc_ref[0:n_keep_c] = hist_ref[nj:CONV_HIST]
    newp_ref[0:n_keep_p] = phist_ref[nj:POOL_HIST]
    for q in range(N_GRP):
        u = _glu(h, wag_ref[q])
        for j in range(nj):
            newc_ref[n_keep_c + j, :, q * GRP:(q + 1) * GRP] = u[j * S:(j + 1) * S]
    for hh in range(N_GRP // 2):
        xb = jnp.dot(h, wag_ref[N_GRP + hh], preferred_element_type=F32)
        for j in range(nj):
            newp_ref[n_keep_p + j, :, hh * 2 * GRP:(hh + 1) * 2 * GRP] = xb[j * S:(j + 1) * S]

    for lt in range(N_CTILE):
        lanes = _lane_tile(lt)
        bias = jnp.broadcast_to(bdw_ref[:, lanes], (S, LANES))
        acc = [bias for _ in range(nj)]
        for kk in range(CONV_HIST + nj):
            row = hist_ref[kk, :, lanes] if kk < CONV_HIST else newc_ref[kk - nj, :, lanes]
            for j in range(nj):
                k = kk - j
                if 0 <= k < CONV_WIDTH:
                    acc[j] = acc[j] + row * wdw_ref[k:k + 1, lanes]
        for j in range(nj):
            c = _head_ln_silu(acc[j], lng_ref[:, lanes], lnb_ref[:, lanes])
            cp[j * S:(j + 1) * S, lanes] = c.astype(BF16)

    for gi, w in enumerate(POOL_WINDOWS):
        sl = slice(gi * GRP, (gi + 1) * GRP)
        rows = ([phist_ref[i, :, sl] for i in range(POOL_HIST - w + 1, POOL_HIST)]
                + [newp_ref[n_keep_p + j, :, sl] for j in range(nj)])
        ds = []
        for j in range(nj):
            win_rows = rows[j:j + w]
            s = win_rows[0]
            for r in win_rows[1:]:
                s = s + r
            ds.append(s * (1.0 / w) - win_rows[-1])
        d = jnp.concatenate(ds, axis=0).astype(BF16)
        p = jnp.dot(d, wpool_ref[gi], preferred_element_type=F32) * pscale_ref[:, sl]
        cp[:, C_CONV + gi * GRP:C_CONV + (gi + 1) * GRP] = p.astype(BF16)

    y = jnp.dot(cp[...], wout_ref[...], preferred_element_type=F32)
    x1 = x + y
    for j in range(nj):
        x1_ref[j] = x1[j * S:(j + 1) * S]


def _mixer_sample(xs_t, hist, phist, mix_w):
    nj, ns, _ = xs_t.shape
    S = S_SMP
    assert ns % S == 0 and hist.shape[0] == CONV_HIST and phist.shape[0] == POOL_HIST, (ns, hist.shape, phist.shape)
    return pl.pallas_call(
        _mixer_sample_kernel,
        grid=(ns // S,),
        in_specs=[
            pl.BlockSpec((nj, S, D_MODEL), lambda s: (0, s, 0)),
            pl.BlockSpec((CONV_HIST, S, C_CONV), lambda s: (0, s, 0)),
            pl.BlockSpec((POOL_HIST, S, C_POOL), lambda s: (0, s, 0)),
        ] + _mixer_weight_specs(),
        out_specs=[
            pl.BlockSpec((nj, S, D_MODEL), lambda s: (0, s, 0)),
            pl.BlockSpec((CONV_HIST, S, C_CONV), lambda s: (0, s, 0)),
            pl.BlockSpec((POOL_HIST, S, C_POOL), lambda s: (0, s, 0)),
        ],
        out_shape=[
            jax.ShapeDtypeStruct((nj, ns, D_MODEL), F32),
            jax.ShapeDtypeStruct((CONV_HIST, ns, C_CONV), F32),
            jax.ShapeDtypeStruct((POOL_HIST, ns, C_POOL), F32),
        ],
        scratch_shapes=[pltpu.VMEM((nj * S, C_CONV + C_POOL), BF16)],
        compiler_params=pltpu.CompilerParams(
            dimension_semantics=("arbitrary",), vmem_limit_bytes=VMEM_LIMIT),
        name="mixer_sample",
    )(xs_t, hist, phist, *mix_w)


def _mlp_kernel(xp_ref, xs_ref, gffn_ref, wup_ref, wdown_ref, gfin_ref, op_ref, os_ref, h_ref):
    j = pl.program_id(1)
    last = pl.num_programs(1) - 1
    tp = xp_ref.shape[0]

    def ffn_chunk(h):
        a = jnp.maximum(jnp.dot(h, wup_ref[...], preferred_element_type=F32), 0.0)
        return jnp.dot((a * a).astype(BF16), wdown_ref[...], preferred_element_type=F32)

    @pl.when(j == 0)
    def _():
        xp = xp_ref[...]
        xs = xs_ref[...]
        h = jnp.concatenate([_rmsnorm(xp, gffn_ref[...]), _rmsnorm(xs, gffn_ref[...])], axis=0).astype(BF16)
        h_ref[...] = h
        d = ffn_chunk(h)
        op_ref[...] = xp + d[:tp]
        os_ref[...] = xs + d[tp:]

    @pl.when((j > 0) & (j < last))
    def _():
        d = ffn_chunk(h_ref[...])
        op_ref[...] += d[:tp]
        os_ref[...] += d[tp:]

    @pl.when(j == last)
    def _():
        d = ffn_chunk(h_ref[...])
        op_ref[...] = _rmsnorm(op_ref[...] + d[:tp], gfin_ref[...])
        os_ref[...] = _rmsnorm(os_ref[...] + d[tp:], gfin_ref[...])


def _mlp(x1p, x1s, gffn, wup, wdown, gfin):
    n_p, _ = x1p.shape
    n_s, _ = x1s.shape
    n_tiles = n_p // TM_MLP
    ts = n_s // n_tiles
    assert n_p == n_tiles * TM_MLP and n_s == n_tiles * ts and ts % (2 * SUBLANES) == 0, (n_p, n_s)
    return pl.pallas_call(
        _mlp_kernel,
        grid=(n_tiles, D_FF // TF_MLP),
        in_specs=[
            pl.BlockSpec((TM_MLP, D_MODEL), lambda i, j: (i, 0)),
            pl.BlockSpec((ts, D_MODEL), lambda i, j: (i, 0)),
            _const_spec((1, D_MODEL)),
            pl.BlockSpec((D_MODEL, TF_MLP), lambda i, j: (0, j)),
            pl.BlockSpec((TF_MLP, D_MODEL), lambda i, j: (j, 0)),
            _const_spec((1, D_MODEL)),
        ],
        out_specs=[
            pl.BlockSpec((TM_MLP, D_MODEL), lambda i, j: (i, 0)),
            pl.BlockSpec((ts, D_MODEL), lambda i, j: (i, 0)),
        ],
        out_shape=[
            jax.ShapeDtypeStruct((n_p, D_MODEL), F32),
            jax.ShapeDtypeStruct((n_s, D_MODEL), F32),
        ],
        scratch_shapes=[pltpu.VMEM((TM_MLP + ts, D_MODEL), BF16)],
        compiler_params=pltpu.CompilerParams(
            dimension_semantics=("arbitrary", "arbitrary"), vmem_limit_bytes=VMEM_LIMIT),
        name="mlp",
    )(x1p, x1s, gffn, wup, wdown, gfin)


def kernel(x_prompt, x_sample, state_conv, state_pool, meta_tokens, norm_mix_g, w_in, w_dw, b_dw, conv_ln_g, conv_ln_b, w_pool, pool_scale, w_out, norm_ffn_g, w_up, w_down, final_norm_g):
    assert norm_mix_g.shape[0] == 1, "single-layer step"
    nb, seq, _ = x_prompt.shape
    ns, nj, _ = x_sample.shape

    gmix = norm_mix_g[0][None]
    gffn = norm_ffn_g[0][None]
    gfin = final_norm_g[None]
    wag_b, wpool_b, wout_b = _regroup_win(w_in[0], w_pool[0], w_out[0])
    mix_w = (gmix, wag_b, w_dw[0], b_dw[0][None],
             conv_ln_g[0][None], conv_ln_b[0][None], wpool_b, pool_scale[0][None], wout_b)

    x1p, utail, xbtail, wup_b, wdown_b = _mixer_prompt(x_prompt, meta_tokens, w_up[0], w_down[0], mix_w)
    new_conv_prompt = utail[:, CONV_HALO - CONV_HIST:][None]
    new_pool_prompt = xbtail[:, POOL_HALO - POOL_HIST:][None]

    xs_t = jnp.transpose(x_sample, (1, 0, 2))
    hist_t = jnp.transpose(state_conv[0], (1, 0, 2))
    phist_t = jnp.transpose(state_pool[0], (1, 0, 2))
    x1s_t, newc_t, newp_t = _mixer_sample(xs_t, hist_t, phist_t, mix_w)

    yp, ys = _mlp(x1p.reshape(nb * seq, D_MODEL), x1s_t.reshape(nj * ns, D_MODEL), gffn, wup_b, wdown_b, gfin)
    y_prompt = yp.reshape(nb, seq, D_MODEL)
    y_sample = jnp.transpose(ys.reshape(nj, ns, D_MODEL), (1, 0, 2))
    new_conv_sample = jnp.transpose(newc_t, (1, 0, 2))[None]
    new_pool_sample = jnp.transpose(newp_t, (1, 0, 2))[None]
    return (y_prompt, y_sample, new_conv_prompt, new_pool_prompt, new_conv_sample, new_pool_sample)
```
